```python
import jax, jax.numpy as jnp
from jax import lax
import numpy as np

D_MODEL = 1024
BATCH = 2
SEQ = 8192
DEPTH = 2
DEC_BATCH = 128
DEC_SEQ = 4
PAST_LEN = 16384
PAGE_SIZE = 128

MLA_HEADS = 4
MLA_Q_RANK = 256
MLA_KV_RANK = 128
MLA_NOPE = 64
MLA_ROPE = 32
MLA_V = 64
MLA_SCALE = (MLA_NOPE + MLA_ROPE) ** -0.5
ROPE_THETA = 10000.0
FOX_HEADS = 4
FOX_DH = 64
FOX_SCALE = FOX_DH ** -0.5
GLA_HEADS = 4
GLA_DK = 32
GLA_DV = 64
GLA_GATE_RANK = 16
GLA_TAU = 16.0
GLA_CHUNK = 32
ML_HEADS = 4
ML_DH = 64
ML_WIDTH = ML_HEADS * ML_DH
ML_CONV = 4
ML_CHUNK = 64
N_BRANCH = 4
BRANCH_W = MLA_HEADS * MLA_V
N_EXPERTS = 64
TOP_K = 6
N_GROUPS = 8
TOPK_GROUPS = 4
EXPERT_FF = 256
SHARED_FF = 256
ROUTED_SCALE = 2.5
MOE_BLOCK = 128
ALPHA = (2 * DEPTH) ** 0.25
BETA = (8 * DEPTH) ** -0.25
Q_BLOCK = 128
EPS = 1e-6
NEG_INF = -1e30

IN_SPLITS = (
    ('mla_cq', MLA_Q_RANK), ('mla_ckv', MLA_KV_RANK), ('mla_krope', MLA_ROPE),
    ('fox_q', FOX_HEADS * FOX_DH), ('fox_k', FOX_HEADS * FOX_DH), ('fox_v', FOX_HEADS * FOX_DH),
    ('fox_f', FOX_HEADS),
    ('gla_q', GLA_HEADS * GLA_DK), ('gla_k', GLA_HEADS * GLA_DK), ('gla_v', GLA_HEADS * GLA_DV),
    ('gla_a', GLA_GATE_RANK), ('gla_g', GLA_HEADS * GLA_DV),
    ('ml_qk', ML_WIDTH), ('ml_v', ML_WIDTH), ('ml_o', ML_WIDTH), ('ml_i', ML_HEADS), ('ml_f', ML_HEADS),
)
D_IN = sum(w for _, w in IN_SPLITS)

kernel_name = 'hybrid_mla_fox_gla_mlstm_moe_step'


def split_in(z):
    parts = {}
    off = 0
    for name, width in IN_SPLITS:
        parts[name] = z[..., off:off + width]
        off += width
    return parts


def pick_chunk(t, c):
    return c if t % c == 0 else t


def rmsnorm(x, g):
    xf = x.astype(jnp.float32)
    y = xf * lax.rsqrt(jnp.mean(xf * xf, axis=-1, keepdims=True) + EPS)
    return (y * g.astype(jnp.float32)).astype(x.dtype)


def layernorm(x, g, b):
    xf = x.astype(jnp.float32)
    mu = jnp.mean(xf, axis=-1, keepdims=True)
    xc = xf - mu
    var = jnp.mean(xc * xc, axis=-1, keepdims=True)
    return (xc * lax.rsqrt(var + EPS) * g.astype(jnp.float32) + b.astype(jnp.float32)).astype(x.dtype)


def rope(x, pos):
    half = MLA_ROPE // 2
    inv = ROPE_THETA ** (-jnp.arange(half, dtype=jnp.float32) / half)
    ang = pos.astype(jnp.float32)[:, None] * inv[None, :]
    shape = (1, pos.shape[0]) + (1,) * (x.ndim - 3) + (half,)
    cos = jnp.cos(ang).reshape(shape)
    sin = jnp.sin(ang).reshape(shape)
    xf = x.astype(jnp.float32)
    x1, x2 = xf[..., :half], xf[..., half:]
    return jnp.concatenate([x1 * cos - x2 * sin, x1 * sin + x2 * cos], axis=-1).astype(x.dtype)


def causal_conv(x, buf, w, b):
    t = x.shape[1]
    xp = jnp.concatenate([buf.astype(x.dtype), x], axis=1)
    y = b + sum(xp[:, j:j + t] * w[j] for j in range(ML_CONV))
    return y, xp[:, t:]


def project_mixers(x, pos, lw):
    n, t, _ = x.shape
    f32 = jnp.float32
    pr = split_in(x @ lw['w_in'])
    cq = rmsnorm(pr['mla_cq'], lw['mla_q_norm'])
    q = (cq @ lw['mla_w_uq']).reshape(n, t, MLA_HEADS, MLA_NOPE + MLA_ROPE)
    return {
        'q_lat': jnp.einsum('nthd,chd->nthc', q[..., :MLA_NOPE], lw['mla_w_uk']),
        'q_pe': rope(q[..., MLA_NOPE:], pos),
        'ckv': rmsnorm(pr['mla_ckv'], lw['mla_kv_norm']),
        'k_pe': rope(pr['mla_krope'], pos),
        'fox_q': pr['fox_q'].reshape(n, t, FOX_HEADS, FOX_DH),
        'fox_k': pr['fox_k'].reshape(n, t, FOX_HEADS, FOX_DH),
        'fox_v': pr['fox_v'].reshape(n, t, FOX_HEADS, FOX_DH),
        'fox_logf': jax.nn.log_sigmoid((pr['fox_f'] + lw['fox_b_f']).astype(f32)),
        'gla_q': pr['gla_q'].reshape(n, t, GLA_HEADS, GLA_DK) * (GLA_DK ** -0.5),
        'gla_k': pr['gla_k'].reshape(n, t, GLA_HEADS, GLA_DK),
        'gla_v': pr['gla_v'].reshape(n, t, GLA_HEADS, GLA_DV),
        'gla_loga': jax.nn.log_sigmoid((pr['gla_a'] @ lw['gla_w_a2'] + lw['gla_b_a']).astype(f32)).reshape(n, t, GLA_HEADS, GLA_DK) / GLA_TAU,
        'gla_gate': pr['gla_g'],
        'ml_in': pr['ml_qk'],
        'ml_v': pr['ml_v'].reshape(n, t, ML_HEADS, ML_DH),
        'ml_o': pr['ml_o'],
        'ml_i': (pr['ml_i'] + lw['ml_b_i']).astype(f32),
        'ml_logf': jax.nn.log_sigmoid((pr['ml_f'] + lw['ml_b_f']).astype(f32)),
    }


def mla_prompt_attn(q_lat, q_pe, ckv, k_pe):
    n, t, h, c = q_lat.shape
    nb = t // Q_BLOCK
    qb_lat = q_lat.reshape(n, nb, Q_BLOCK, h, c).swapaxes(0, 1)
    qb_pe = q_pe.reshape(n, nb, Q_BLOCK, h, MLA_ROPE).swapaxes(0, 1)
    kpos = jnp.arange(t)

    def block(args):
        ql, qp, i = args
        s = (jnp.einsum('nqhc,nkc->nhqk', ql, ckv) + jnp.einsum('nqhr,nkr->nhqk', qp, k_pe)).astype(jnp.float32) * MLA_SCALE
        qpos = i * Q_BLOCK + jnp.arange(Q_BLOCK)
        s = jnp.where(kpos[None, :] <= qpos[:, None], s, NEG_INF)
        p = jax.nn.softmax(s, axis=-1).astype(ckv.dtype)
        return jnp.einsum('nhqk,nkc->nqhc', p, ckv)

    o = lax.map(block, (qb_lat, qb_pe, jnp.arange(nb)))
    return o.swapaxes(0, 1).reshape(n, t, h, c)


def mla_sample_attn(q_lat, q_pe, ckv, k_pe, cache_lat, cache_kpe, layer, page_table):
    s_len = q_lat.shape[1]
    past = page_table.shape[1] * PAGE_SIZE
    mask = jnp.arange(past + s_len)[None, :] <= past + jnp.arange(s_len)[:, None]

    def one(args):
        ql, qp, cn, kn, pt = args
        lat = jnp.concatenate([cache_lat[layer, pt].reshape(past, MLA_KV_RANK).astype(cn.dtype), cn], axis=0)
        kpe = jnp.concatenate([cache_kpe[layer, pt].reshape(past, MLA_ROPE).astype(kn.dtype), kn], axis=0)
        s = (jnp.einsum('qhc,kc->hqk', ql, lat) + jnp.einsum('qhr,kr->hqk', qp, kpe)).astype(jnp.float32) * MLA_SCALE
        s = jnp.where(mask, s, NEG_INF)
        p = jax.nn.softmax(s, axis=-1).astype(lat.dtype)
        return jnp.einsum('hqk,kc->qhc', p, lat)

    return lax.map(one, (q_lat, q_pe, ckv, k_pe, page_table))


def fox_prompt_attn(q, k, v, logf):
    n, t, h, dh = q.shape
    nb = t // Q_BLOCK
    c = jnp.cumsum(logf, axis=1)
    qb = q.reshape(n, nb, Q_BLOCK, h, dh).swapaxes(0, 1)
    cb = c.reshape(n, nb, Q_BLOCK, h).swapaxes(0, 1)
    kterm = -jnp.transpose(c, (0, 2, 1))[:, :, None, :]
    kpos = jnp.arange(t)

    def block(args):
        qi, ci, i = args
        s = jnp.einsum('nqhd,nkhd->nhqk', qi, k).astype(jnp.float32) * FOX_SCALE + jnp.transpose(ci, (0, 2, 1))[..., None] + kterm
        qpos = i * Q_BLOCK + jnp.arange(Q_BLOCK)
        s = jnp.where(kpos[None, :] <= qpos[:, None], s, NEG_INF)
        p = jax.nn.softmax(s, axis=-1).astype(v.dtype)
        return jnp.einsum('nhqk,nkhd->nqhd', p, v)

    o = lax.map(block, (qb, cb, jnp.arange(nb)))
    return o.swapaxes(0, 1).reshape(n, t, h, dh)


def fox_sample_attn(q, k, v, logf, cache_k, cache_v, cache_logf, layer, page_table):
    s_len = q.shape[1]
    past = page_table.shape[1] * PAGE_SIZE
    mask = jnp.arange(past + s_len)[None, :] <= past + jnp.arange(s_len)[:, None]

    def one(args):
        qi, ki, vi, fi, pt = args
        pk = cache_k[layer, pt].reshape(past, FOX_HEADS, FOX_DH).astype(ki.dtype)
        pv = cache_v[layer, pt].reshape(past, FOX_HEADS, FOX_DH).astype(vi.dtype)
        pf = cache_logf[layer, pt].reshape(past, FOX_HEADS).astype(jnp.float32)
        suffix = lax.cumsum(pf, axis=0, reverse=True) - pf
        cum_new = jnp.cumsum(fi, axis=0)
        kterm = jnp.concatenate([suffix, -cum_new], axis=0)
        keys = jnp.concatenate([pk, ki], axis=0)
        vals = jnp.concatenate([pv, vi], axis=0)
        s = jnp.einsum('qhd,khd->hqk', qi, keys).astype(jnp.float32) * FOX_SCALE + cum_new.T[:, :, None] + kterm.T[:, None, :]
        s = jnp.where(mask, s, NEG_INF)
        p = jax.nn.softmax(s, axis=-1).astype(vals.dtype)
        return jnp.einsum('hqk,khd->qhd', p, vals)

    return lax.map(one, (q, k, v, logf, page_table))


def gla_recurrence(q, k, v, log_a, s0):
    n, t, h, dk = q.shape
    dv = v.shape[-1]
    L = pick_chunk(t, GLA_CHUNK)
    nc = t // L
    f32 = jnp.float32
    tril = jnp.tril(jnp.ones((L, L), bool))

    def chunks(z):
        return z.astype(f32).reshape((n, nc, L) + z.shape[2:]).swapaxes(0, 1)

    def step(S, xs):
        qc, kc, vc, ac = xs
        b = jnp.cumsum(ac, axis=1)
        diff = b[:, :, None] - b[:, None, :]
        decay = jnp.exp(jnp.where(tril[None, :, :, None, None], diff, -jnp.inf))
        a_mat = jnp.einsum('nthk,nshk,ntshk->nhts', qc, kc, decay)
        o = jnp.einsum('nhts,nshv->nthv', a_mat, vc) + jnp.einsum('nthk,nhkv->nthv', qc * jnp.exp(b), S)
        b_last = b[:, -1]
        S_new = S * jnp.exp(b_last)[..., None] + jnp.einsum('nshk,nshv->nhkv', kc * jnp.exp(b_last[:, None] - b), vc)
        return S_new, o

    S, o = lax.scan(step, s0.astype(f32), (chunks(q), chunks(k), chunks(v), chunks(log_a)))
    return o.swapaxes(0, 1).reshape(n, t, h, dv), S


def mlstm_qk(ml_in, buf, lw):
    n, t, _ = ml_in.shape
    y, new_buf = causal_conv(ml_in, buf, lw['ml_conv_w'], lw['ml_conv_b'])
    c = jax.nn.silu(y).reshape(n, t, ML_HEADS, ML_DH)
    q = jnp.einsum('nthd,hde->nthe', c, lw['ml_w_q'])
    k = jnp.einsum('nthd,hde->nthe', c, lw['ml_w_k']) * (ML_DH ** -0.5)
    return q, k, new_buf


def mlstm_recurrence(q, k, v, i_pre, logf, c0, n0, m0):
    nbat, t, h, d = q.shape
    L = pick_chunk(t, ML_CHUNK)
    nc = t // L
    f32 = jnp.float32
    tril = jnp.tril(jnp.ones((L, L), bool))

    def chunks(z):
        return z.astype(f32).reshape((nbat, nc, L) + z.shape[2:]).swapaxes(0, 1)

    def step(carry, xs):
        C, nv, m = carry
        qc, kc, vc, ic, fc = xs
        F = jnp.cumsum(fc, axis=1)
        a = F + m[:, None, :]
        dmat = jnp.where(tril[None, :, :, None], F[:, :, None, :] - F[:, None, :, :] + ic[:, None, :, :], -jnp.inf)
        m_t = jnp.maximum(a, jnp.max(dmat, axis=2))
        w_state = jnp.exp(a - m_t)
        w_in = jnp.exp(dmat - m_t[:, :, None, :])
        qk = jnp.einsum('bthd,bshd->btsh', qc, kc) * w_in
        num = jnp.einsum('btsh,bshv->bthv', qk, vc) + w_state[..., None] * jnp.einsum('bthd,bhdv->bthv', qc, C)
        den = jnp.sum(qk, axis=2) + w_state * jnp.einsum('bthd,bhd->bth', qc, nv)
        h_out = num / jnp.maximum(jnp.abs(den), jnp.exp(-m_t))[..., None]
        wl_state, wl_in = w_state[:, -1], w_in[:, -1]
        C_new = wl_state[:, :, None, None] * C + jnp.einsum('bsh,bshd,bshv->bhdv', wl_in, kc, vc)
        n_new = wl_state[..., None] * nv + jnp.einsum('bsh,bshd->bhd', wl_in, kc)
        return (C_new, n_new, m_t[:, -1]), h_out

    carry0 = (c0.astype(f32), n0.astype(f32), m0.astype(f32))
    state, hs = lax.scan(step, carry0, (chunks(q), chunks(k), chunks(v), chunks(i_pre), chunks(logf)))
    return hs.swapaxes(0, 1).reshape(nbat, t, h, d), state


def moe_block(x, lw):
    m = x.shape[0]
    f32 = jnp.float32
    scores = jax.nn.sigmoid((x @ lw['router_w']).astype(f32))
    biased = scores + lw['router_bias'].astype(f32)
    grouped = biased.reshape(m, N_GROUPS, N_EXPERTS // N_GROUPS)
    group_score = jnp.sum(lax.top_k(grouped, 2)[0], axis=-1)
    _, gidx = lax.top_k(group_score, TOPK_GROUPS)
    gmask = jnp.any(gidx[:, :, None] == jnp.arange(N_GROUPS)[None, None, :], axis=1)
    emask = jnp.repeat(gmask, N_EXPERTS // N_GROUPS, axis=1)
    _, eidx = lax.top_k(jnp.where(emask, biased, -jnp.inf), TOP_K)
    w = jnp.take_along_axis(scores, eidx, axis=1)
    w = w / jnp.sum(w, axis=-1, keepdims=True) * ROUTED_SCALE
    gates = jnp.sum(jax.nn.one_hot(eidx, N_EXPERTS, dtype=f32) * w[..., None], axis=1).astype(x.dtype)
    hid = jax.nn.silu(jnp.einsum('md,edf->mef', x, lw['exp_w_gate'])) * jnp.einsum('md,edf->mef', x, lw['exp_w_up'])
    routed = jnp.einsum('mef,efd->md', hid * gates[:, :, None], lw['exp_w_down'])
    shared = (jax.nn.silu(x @ lw['sh_w_gate']) * (x @ lw['sh_w_up'])) @ lw['sh_w_down']
    return routed + shared


def moe_ffn(x, lw):
    n, t, d = x.shape
    m = n * t
    blk = MOE_BLOCK if m % MOE_BLOCK == 0 else m
    out = lax.map(lambda xb: moe_block(xb, lw), x.reshape(m // blk, blk, d))
    return out.reshape(n, t, d)


def merge_and_ffn(x, mla_lat, fox_o, gla_o, ml_h, pj, lw):
    n, t, d = x.shape
    dt = x.dtype
    mla = jnp.einsum('nthc,chv->nthv', mla_lat, lw['mla_w_uv']).reshape(n, t, BRANCH_W)
    fox = fox_o.reshape(n, t, BRANCH_W)
    gla = rmsnorm(gla_o.astype(dt), lw['gla_norm']).reshape(n, t, BRANCH_W) * jax.nn.silu(pj['gla_gate'])
    ml = rmsnorm(ml_h.astype(dt), lw['ml_norm']).reshape(n, t, BRANCH_W) * jax.nn.sigmoid(pj['ml_o'])
    branches = jnp.stack([mla, fox, gla, ml], axis=2)
    proj = jnp.einsum('ntbi,bid->ntbd', branches, lw['w_branch'])
    gates = jax.nn.sigmoid(x @ lw['w_gate'] + lw['b_gate']).reshape(n, t, N_BRANCH, d)
    mix = jnp.sum(gates * proj, axis=2) @ lw['w_out']
    h = layernorm(ALPHA * x + mix, lw['ln1_g'], lw['ln1_b'])
    return layernorm(ALPHA * h + moe_ffn(h, lw), lw['ln2_g'], lw['ln2_b'])


def prompt_layer(x, lw):
    n, t, _ = x.shape
    dt = x.dtype
    f32 = jnp.float32
    pj = project_mixers(x, jnp.arange(t), lw)
    mla_lat = mla_prompt_attn(pj['q_lat'], pj['q_pe'], pj['ckv'], pj['k_pe'])
    fox_o = fox_prompt_attn(pj['fox_q'], pj['fox_k'], pj['fox_v'], pj['fox_logf'])
    gla_o, gla_s = gla_recurrence(pj['gla_q'], pj['gla_k'], pj['gla_v'], pj['gla_loga'], jnp.zeros((n, GLA_HEADS, GLA_DK, GLA_DV), f32))
    ml_q, ml_k, ml_buf = mlstm_qk(pj['ml_in'], jnp.zeros((n, ML_CONV - 1, ML_WIDTH), dt), lw)
    ml_h, (mc, mn, mm) = mlstm_recurrence(ml_q, ml_k, pj['ml_v'], pj['ml_i'], pj['ml_logf'],
                                          jnp.zeros((n, ML_HEADS, ML_DH, ML_DH), f32),
                                          jnp.zeros((n, ML_HEADS, ML_DH), f32),
                                          jnp.zeros((n, ML_HEADS), f32))
    y = merge_and_ffn(x, mla_lat, fox_o, gla_o, ml_h, pj, lw)
    return y, (pj['ckv'], pj['k_pe'], pj['fox_k'], pj['fox_v'], pj['fox_logf'].astype(dt),
               gla_s.astype(dt), mc.astype(dt), mn.astype(dt), mm.astype(dt), ml_buf)


def sample_layer(x, l, lw, cache_mla_latent, cache_mla_krope, cache_fox_k, cache_fox_v,
                 cache_fox_logf, state_gla, state_mlstm_c, state_mlstm_n, state_mlstm_m,
                 state_mlstm_conv, page_table):
    t = x.shape[1]
    dt = x.dtype
    pos = page_table.shape[1] * PAGE_SIZE + jnp.arange(t)
    pj = project_mixers(x, pos, lw)
    mla_lat = mla_sample_attn(pj['q_lat'], pj['q_pe'], pj['ckv'], pj['k_pe'], cache_mla_latent, cache_mla_krope, l, page_table)
    fox_o = fox_sample_attn(pj['fox_q'], pj['fox_k'], pj['fox_v'], pj['fox_logf'], cache_fox_k, cache_fox_v, cache_fox_logf, l, page_table)
    gla_o, gla_s = gla_recurrence(pj['gla_q'], pj['gla_k'], pj['gla_v'], pj['gla_loga'], state_gla[l])
    ml_q, ml_k, ml_buf = mlstm_qk(pj['ml_in'], state_mlstm_conv[l], lw)
    ml_h, (mc, mn, mm) = mlstm_recurrence(ml_q, ml_k, pj['ml_v'], pj['ml_i'], pj['ml_logf'],
                                          state_mlstm_c[l], state_mlstm_n[l], state_mlstm_m[l])
    y = merge_and_ffn(x, mla_lat, fox_o, gla_o, ml_h, pj, lw)
    return y, (pj['ckv'], pj['k_pe'], pj['fox_k'], pj['fox_v'], pj['fox_logf'].astype(dt),
               gla_s.astype(dt), mc.astype(dt), mn.astype(dt), mm.astype(dt), ml_buf)


def setup_inputs(seed: int = 0) -> dict:
    key = jax.random.key(seed)
    keys = iter(jax.random.split(key, 64))

    def nrm(shape, scale=1.0):
        return jax.random.normal(next(keys), shape, jnp.float32) * scale

    n_pages = PAST_LEN // PAGE_SIZE
    n_used = DEC_BATCH * n_pages
    n_pool = n_used + max(1, n_used // 4)
    L = DEPTH
    D = D_MODEL
    page_table = jax.random.permutation(next(keys), n_pool)[:n_used].reshape(DEC_BATCH, n_pages).astype(jnp.int32)
    return {
        'x_prompt': nrm((BATCH, SEQ, D)),
        'x_sample': nrm((DEC_BATCH, DEC_SEQ, D)),
        'cache_mla_latent': nrm((L, n_pool, PAGE_SIZE, MLA_KV_RANK)),
        'cache_mla_krope': nrm((L, n_pool, PAGE_SIZE, MLA_ROPE)),
        'cache_fox_k': nrm((L, n_pool, PAGE_SIZE, FOX_HEADS, FOX_DH)),
        'cache_fox_v': nrm((L, n_pool, PAGE_SIZE, FOX_HEADS, FOX_DH)),
        'cache_fox_logf': jax.nn.log_sigmoid(nrm((L, n_pool, PAGE_SIZE, FOX_HEADS)) + 3.0),
        'state_gla': nrm((L, DEC_BATCH, GLA_HEADS, GLA_DK, GLA_DV), 0.5),
        'state_mlstm_c': nrm((L, DEC_BATCH, ML_HEADS, ML_DH, ML_DH), 0.5),
        'state_mlstm_n': nrm((L, DEC_BATCH, ML_HEADS, ML_DH), 0.5),
        'state_mlstm_m': nrm((L, DEC_BATCH, ML_HEADS)),
        'state_mlstm_conv': nrm((L, DEC_BATCH, ML_CONV - 1, ML_WIDTH)),
        'page_table': page_table,
        'w_in': nrm((L, D, D_IN), D ** -0.5),
        'mla_q_norm': 1.0 + nrm((L, MLA_Q_RANK), 0.02),
        'mla_w_uq': nrm((L, MLA_Q_RANK, MLA_HEADS * (MLA_NOPE + MLA_ROPE)), MLA_Q_RANK ** -0.5),
        'mla_kv_norm': 1.0 + nrm((L, MLA_KV_RANK), 0.02),
        'mla_w_uk': nrm((L, MLA_KV_RANK, MLA_HEADS, MLA_NOPE), MLA_KV_RANK ** -0.5),
        'mla_w_uv': nrm((L, MLA_KV_RANK, MLA_HEADS, MLA_V), MLA_KV_RANK ** -0.5),
        'fox_b_f': 3.0 + nrm((L, FOX_HEADS), 0.1),
        'gla_w_a2': nrm((L, GLA_GATE_RANK, GLA_HEADS * GLA_DK), GLA_GATE_RANK ** -0.5),
        'gla_b_a': nrm((L, GLA_HEADS * GLA_DK), 0.1),
        'gla_norm': 1.0 + nrm((L, GLA_DV), 0.02),
        'ml_conv_w': nrm((L, ML_CONV, ML_WIDTH), ML_CONV ** -0.5),
        'ml_conv_b': nrm((L, ML_WIDTH), 0.02),
        'ml_w_q': nrm((L, ML_HEADS, ML_DH, ML_DH), ML_DH ** -0.5),
        'ml_w_k': nrm((L, ML_HEADS, ML_DH, ML_DH), ML_DH ** -0.5),
        'ml_b_i': nrm((L, ML_HEADS), 0.1),
        'ml_b_f': 3.0 + 3.0 * jax.random.uniform(next(keys), (L, ML_HEADS), jnp.float32),
        'ml_norm': 1.0 + nrm((L, ML_DH), 0.02),
        'w_branch': nrm((L, N_BRANCH, BRANCH_W, D), BRANCH_W ** -0.5 * BETA),
        'w_gate': nrm((L, D, N_BRANCH * D), D ** -0.5),
        'b_gate': nrm((L, N_BRANCH * D), 0.02),
        'w_out': nrm((L, D, D), D ** -0.5 * BETA),
        'ln1_g': 1.0 + nrm((L, D), 0.02),
        'ln1_b': nrm((L, D), 0.02),
        'router_w': nrm((L, D, N_EXPERTS), D ** -0.5),
        'router_bias': nrm((L, N_EXPERTS), 0.01),
        'exp_w_gate': nrm((L, N_EXPERTS, D, EXPERT_FF), D ** -0.5),
        'exp_w_up': nrm((L, N_EXPERTS, D, EXPERT_FF), D ** -0.5),
        'exp_w_down': nrm((L, N_EXPERTS, EXPERT_FF, D), EXPERT_FF ** -0.5 * BETA),
        'sh_w_gate': nrm((L, D, SHARED_FF), D ** -0.5),
        'sh_w_up': nrm((L, D, SHARED_FF), D ** -0.5),
        'sh_w_down': nrm((L, SHARED_FF, D), SHARED_FF ** -0.5 * BETA),
        'ln2_g': 1.0 + nrm((L, D), 0.02),
        'ln2_b': nrm((L, D), 0.02),
    }


def reference(x_prompt, x_sample, cache_mla_latent, cache_mla_krope, cache_fox_k, cache_fox_v,
              cache_fox_logf, state_gla, state_mlstm_c, state_mlstm_n, state_mlstm_m,
              state_mlstm_conv, page_table, w_in, mla_q_norm, mla_w_uq, mla_kv_norm, mla_w_uk,
              mla_w_uv, fox_b_f, gla_w_a2, gla_b_a, gla_norm, ml_conv_w, ml_conv_b, ml_w_q,
              ml_w_k, ml_b_i, ml_b_f, ml_norm, w_branch, w_gate, b_gate, w_out, ln1_g, ln1_b,
              router_w, router_bias, exp_w_gate, exp_w_up, exp_w_down, sh_w_gate, sh_w_up,
              sh_w_down, ln2_g, ln2_b):
    xp, xs = x_prompt, x_sample
    p_states = [[] for _ in range(10)]
    s_states = [[] for _ in range(10)]
    for l in range(DEPTH):
        lw = {
            'w_in': w_in[l], 'mla_q_norm': mla_q_norm[l], 'mla_w_uq': mla_w_uq[l],
            'mla_kv_norm': mla_kv_norm[l], 'mla_w_uk': mla_w_uk[l], 'mla_w_uv': mla_w_uv[l],
            'fox_b_f': fox_b_f[l], 'gla_w_a2': gla_w_a2[l], 'gla_b_a': gla_b_a[l],
            'gla_norm': gla_norm[l], 'ml_conv_w': ml_conv_w[l], 'ml_conv_b': ml_conv_b[l],
            'ml_w_q': ml_w_q[l], 'ml_w_k': ml_w_k[l], 'ml_b_i': ml_b_i[l], 'ml_b_f': ml_b_f[l],
            'ml_norm': ml_norm[l], 'w_branch': w_branch[l], 'w_gate': w_gate[l],
            'b_gate': b_gate[l], 'w_out': w_out[l], 'ln1_g': ln1_g[l], 'ln1_b': ln1_b[l],
            'router_w': router_w[l], 'router_bias': router_bias[l],
            'exp_w_gate': exp_w_gate[l], 'exp_w_up': exp_w_up[l], 'exp_w_down': exp_w_down[l],
            'sh_w_gate': sh_w_gate[l], 'sh_w_up': sh_w_up[l], 'sh_w_down': sh_w_down[l],
            'ln2_g': ln2_g[l], 'ln2_b': ln2_b[l],
        }
        xp, st_p = prompt_layer(xp, lw)
        xs, st_s = sample_layer(xs, l, lw, cache_mla_latent, cache_mla_krope, cache_fox_k,
                                cache_fox_v, cache_fox_logf, state_gla, state_mlstm_c,
                                state_mlstm_n, state_mlstm_m, state_mlstm_conv, page_table)
        for j in range(10):
            p_states[j].append(st_p[j])
            s_states[j].append(st_s[j])
    (p_lat, p_kpe, p_fk, p_fv, p_ff, p_gla, p_mc, p_mn, p_mm, p_conv) = [jnp.stack(s, axis=0) for s in p_states]
    (s_lat, s_kpe, s_fk, s_fv, s_ff, s_gla, s_mc, s_mn, s_mm, s_conv) = [jnp.stack(s, axis=0) for s in s_states]
    return (xp, xs, p_lat, p_kpe, p_fk, p_fv, p_ff, p_gla, p_mc, p_mn, p_mm, p_conv,
            s_lat, s_kpe, s_fk, s_fv, s_ff, s_gla, s_mc, s_mn, s_mm, s_conv)
```

```python
import functools

import jax
import jax.numpy as jnp
from jax import lax
from jax.experimental import pallas as pl
from jax.experimental.pallas import tpu as pltpu

D_MODEL = 1024
DEPTH = 2
PAGE_SIZE = 128
MLA_HEADS = 4
MLA_Q_RANK = 256
MLA_KV_RANK = 128
MLA_NOPE = 64
MLA_ROPE = 32
MLA_V = 64
MLA_SCALE = (MLA_NOPE + MLA_ROPE) ** -0.5
ROPE_THETA = 10000.0
FOX_HEADS = 4
FOX_DH = 64
FOX_SCALE = FOX_DH ** -0.5
GLA_HEADS = 4
GLA_DK = 32
GLA_DV = 64
GLA_GATE_RANK = 16
GLA_TAU = 16.0
GLA_CHUNK = 32
ML_HEADS = 4
ML_DH = 64
ML_WIDTH = ML_HEADS * ML_DH
ML_CONV = 4
ML_CHUNK = 64
N_BRANCH = 4
BRANCH_W = MLA_HEADS * MLA_V
N_EXPERTS = 64
TOP_K = 6
N_GROUPS = 8
GROUP_SIZE = N_EXPERTS // N_GROUPS
TOPK_GROUPS = 4
EXPERT_FF = 256
ROUTED_SCALE = 2.5
ALPHA = (2 * DEPTH) ** 0.25
Q_BLOCK = 128
EPS = 1e-6
NEG_INF = -1e30

IN_SPLITS = (
    ('mla_cq', MLA_Q_RANK), ('mla_ckv', MLA_KV_RANK), ('mla_krope', MLA_ROPE),
    ('fox_q', FOX_HEADS * FOX_DH), ('fox_k', FOX_HEADS * FOX_DH), ('fox_v', FOX_HEADS * FOX_DH),
    ('fox_f', FOX_HEADS),
    ('gla_q', GLA_HEADS * GLA_DK), ('gla_k', GLA_HEADS * GLA_DK), ('gla_v', GLA_HEADS * GLA_DV),
    ('gla_a', GLA_GATE_RANK), ('gla_g', GLA_HEADS * GLA_DV),
    ('ml_qk', ML_WIDTH), ('ml_v', ML_WIDTH), ('ml_o', ML_WIDTH), ('ml_i', ML_HEADS), ('ml_f', ML_HEADS),
)

LANE = 128
QK_PAD = 256
BF16 = jnp.bfloat16
F32 = jnp.float32
VMEM_LIMIT = 56 * 1024 * 1024

_NN = (((1,), (0,)), ((), ()))
_NT = (((1,), (1,)), ((), ()))


def _params(sem):
    return pltpu.CompilerParams(dimension_semantics=sem, vmem_limit_bytes=VMEM_LIMIT)


def _token_tile(m):
    for t in (512, 256, 128):
        if m % t == 0:
            return t
    raise ValueError(f"token count {m} must be a multiple of 128")


def _split(a):
    hi = a.astype(BF16)
    lo = (a - hi.astype(F32)).astype(BF16)
    return hi, lo


def _wstack(w, prec):
    hi = w.astype(BF16)
    if prec == 1:
        return hi[None]
    return jnp.stack([hi, (w - hi.astype(F32)).astype(BF16)])


def _mxu(a, w, prec, dn=_NN):
    a_hi = a.astype(BF16)
    out = lax.dot_general(a_hi, w[0], dn, preferred_element_type=F32)
    if prec == 3:
        a_lo = (a - a_hi.astype(F32)).astype(BF16)
        out = out + lax.dot_general(a_hi, w[1], dn, preferred_element_type=F32)
        out = out + lax.dot_general(a_lo, w[0], dn, preferred_element_type=F32)
    return out


def _store_split(ref, a, prec, idx=()):
    hi = a.astype(BF16)
    ref[(0,) + idx] = hi
    if prec == 3:
        ref[(1,) + idx] = (a - hi.astype(F32)).astype(BF16)


def _rms(x):
    return x * lax.rsqrt(jnp.mean(x * x, axis=-1, keepdims=True) + EPS)


def _log_sigmoid(x):
    return jnp.minimum(x, 0.0) - jnp.log1p(jnp.exp(-jnp.abs(x)))


def _sigmoid(x):
    return 1.0 / (1.0 + jnp.exp(-x))


def _full(shape):
    nd = len(shape)
    return pl.BlockSpec(shape, lambda *_: (0,) * nd)


def _mla_proj_body(prec, x_ref, w_ref, wuq_ref, wuk_ref, gq_ref, gkv_ref, cos_ref, sin_ref,
                   qcat_ref, kcat_ref, ckv_ref, kpe_ref):
    z = _mxu(x_ref[...], w_ref, prec)
    cq = _rms(z[:, :256]) * gq_ref[...]
    ckv = _rms(z[:, 256:384]) * gkv_ref[...]
    cosp, sinp = cos_ref[...], sin_ref[...]
    kpe = z[:, 384:512] * cosp + z[:, 512:640] * sinp
    ckv_ref[...] = ckv
    kpe_ref[...] = kpe
    _store_split(kcat_ref, jnp.concatenate([ckv, kpe], axis=1), prec)
    qq = _mxu(cq, wuq_ref, prec)
    qlat = _mxu(qq[:, :256], wuk_ref, prec)
    for h in range(MLA_HEADS):
        lo, hi = h * LANE, (h + 1) * LANE
        qpe = qq[:, 256 + lo:256 + hi] * cosp + qq[:, 768 + lo:768 + hi] * sinp
        qc = jnp.concatenate([qlat[:, lo:hi], qpe], axis=1) * MLA_SCALE
        _store_split(qcat_ref, qc, prec, (h,))


def _mla_proj(x, lw, cosp, sinp, prec):
    m = x.shape[0]
    tm = _token_tile(m)
    p = 1 if prec == 1 else 2
    half = MLA_ROPE // 2
    w_in = lw['w_in']
    o_cq, o_ckv, o_kr = 0, MLA_Q_RANK, MLA_Q_RANK + MLA_KV_RANK

    def rot_cols(w):
        return jnp.concatenate([-w[..., half:], w[..., :half]], axis=-1)

    def pad_lanes(w):
        return jnp.pad(w, ((0, 0), (0, LANE - w.shape[-1])))

    wkr = w_in[:, o_kr:o_kr + MLA_ROPE]
    w_main = jnp.concatenate([w_in[:, o_cq:o_cq + MLA_Q_RANK], w_in[:, o_ckv:o_ckv + MLA_KV_RANK],
                              pad_lanes(wkr), pad_lanes(rot_cols(wkr))], axis=1)
    wuq = lw['mla_w_uq'].reshape(MLA_Q_RANK, MLA_HEADS, MLA_NOPE + MLA_ROPE)
    w_nope = wuq[:, :, :MLA_NOPE].reshape(MLA_Q_RANK, MLA_HEADS * MLA_NOPE)
    w_rope = wuq[:, :, MLA_NOPE:]
    w_rope_p = jnp.pad(w_rope, ((0, 0), (0, 0), (0, LANE - MLA_ROPE))).reshape(MLA_Q_RANK, MLA_HEADS * LANE)
    w_rot_p = jnp.pad(rot_cols(w_rope), ((0, 0), (0, 0), (0, LANE - MLA_ROPE))).reshape(MLA_Q_RANK, MLA_HEADS * LANE)
    wuq_all = jnp.concatenate([w_nope, w_rope_p, w_rot_p], axis=1)
    wuk = jnp.transpose(lw['mla_w_uk'], (1, 2, 0))
    wuk_bd = jnp.zeros((MLA_HEADS * MLA_NOPE, MLA_HEADS * MLA_KV_RANK), F32)
    for h in range(MLA_HEADS):
        wuk_bd = wuk_bd.at[h * MLA_NOPE:(h + 1) * MLA_NOPE, h * MLA_KV_RANK:(h + 1) * MLA_KV_RANK].set(wuk[h])
    ws = [_wstack(w_main, prec), _wstack(wuq_all, prec), _wstack(wuk_bd, prec)]
    return pl.pallas_call(
        functools.partial(_mla_proj_body, prec),
        grid=(m // tm,),
        in_specs=[pl.BlockSpec((tm, D_MODEL), lambda i: (i, 0))] + [_full(w.shape) for w in ws]
        + [_full((1, MLA_Q_RANK)), _full((1, MLA_KV_RANK)),
           pl.BlockSpec((tm, LANE), lambda i: (i, 0)), pl.BlockSpec((tm, LANE), lambda i: (i, 0))],
        out_specs=[pl.BlockSpec((p, MLA_HEADS, tm, QK_PAD), lambda i: (0, 0, i, 0)),
                   pl.BlockSpec((p, tm, QK_PAD), lambda i: (0, i, 0)),
                   pl.BlockSpec((tm, LANE), lambda i: (i, 0)),
                   pl.BlockSpec((tm, LANE), lambda i: (i, 0))],
        out_shape=[jax.ShapeDtypeStruct((p, MLA_HEADS, m, QK_PAD), BF16),
                   jax.ShapeDtypeStruct((p, m, QK_PAD), BF16),
                   jax.ShapeDtypeStruct((m, LANE), F32),
                   jax.ShapeDtypeStruct((m, LANE), F32)],
        compiler_params=_params(("parallel",)),
        name="mla_proj",
    )(x, *ws, lw['mla_q_norm'][None], lw['mla_kv_norm'][None], cosp, sinp)


def _fox_proj_body(prec, x_ref, w_ref, qbd_ref, kb_ref, vb_ref, k_ref, v_ref):
    z = _mxu(x_ref[...], w_ref, prec)
    w = FOX_HEADS * FOX_DH
    q, k, v = z[:, :w] * FOX_SCALE, z[:, w:2 * w], z[:, 2 * w:]
    k_ref[...] = k
    v_ref[...] = v
    _store_split(kb_ref, k, prec)
    _store_split(vb_ref, v, prec)
    head = lax.broadcasted_iota(jnp.int32, q.shape, 1) // FOX_DH
    for h in range(FOX_HEADS):
        _store_split(qbd_ref, jnp.where(head == h, q, 0.0), prec, (h,))


def _fox_proj(x, lw, prec):
    m = x.shape[0]
    tm = _token_tile(m)
    p = 1 if prec == 1 else 2
    w = FOX_HEADS * FOX_DH
    off = MLA_Q_RANK + MLA_KV_RANK + MLA_ROPE
    ws = _wstack(lw['w_in'][:, off:off + 3 * w], prec)
    tok = lambda i: (i, 0)
    return pl.pallas_call(
        functools.partial(_fox_proj_body, prec),
        grid=(m // tm,),
        in_specs=[pl.BlockSpec((tm, D_MODEL), tok), _full(ws.shape)],
        out_specs=[pl.BlockSpec((p, FOX_HEADS, tm, w), lambda i: (0, 0, i, 0)),
                   pl.BlockSpec((p, tm, w), lambda i: (0, i, 0)),
                   pl.BlockSpec((p, tm, w), lambda i: (0, i, 0)),
                   pl.BlockSpec((tm, w), tok), pl.BlockSpec((tm, w), tok)],
        out_shape=[jax.ShapeDtypeStruct((p, FOX_HEADS, m, w), BF16),
                   jax.ShapeDtypeStruct((p, m, w), BF16), jax.ShapeDtypeStruct((p, m, w), BF16),
                   jax.ShapeDtypeStruct((m, w), F32), jax.ShapeDtypeStruct((m, w), F32)],
        compiler_params=_params(("parallel",)),
        name="fox_proj",
    )(x, ws)


def _rec_proj_body(prec, x_ref, w_ref, wa2_ref, ba_ref,
                   gq_ref, gk_ref, gv_ref, gg_ref, ga_ref, mi_ref, mv_ref, mo_ref):
    z = _mxu(x_ref[...], w_ref, prec)
    gq_ref[...] = z[:, 0:128] * (GLA_DK ** -0.5)
    gk_ref[...] = z[:, 128:256]
    gv_ref[...] = z[:, 256:512]
    gg_ref[...] = z[:, 512:768]
    a = _mxu(z[:, 768:896], wa2_ref, prec) + ba_ref[...]
    ga_ref[...] = _log_sigmoid(a) * (1.0 / GLA_TAU)
    mi_ref[...] = z[:, 896:1152]
    mv_ref[...] = z[:, 1152:1408]
    mo_ref[...] = z[:, 1408:1664]


def _rec_proj(x, lw, prec):
    m = x.shape[0]
    tm = _token_tile(m)
    offs = {}
    off = 0
    for name, width in IN_SPLITS:
        offs[name] = (off, off + width)
        off += width
    w_in = lw['w_in']
    col = lambda n: w_in[:, offs[n][0]:offs[n][1]]
    w_a = jnp.pad(col('gla_a'), ((0, 0), (0, LANE - GLA_GATE_RANK)))
    w_main = jnp.concatenate([col('gla_q'), col('gla_k'), col('gla_v'), col('gla_g'), w_a,
                              col('ml_qk'), col('ml_v'), col('ml_o')], axis=1)
    wa2 = jnp.pad(lw['gla_w_a2'], ((0, LANE - GLA_GATE_RANK), (0, 0)))
    ws = [_wstack(w_main, prec), _wstack(wa2, prec)]
    tok = lambda i: (i, 0)
    widths = (128, 128, 256, 256, 128, 256, 256, 256)
    return pl.pallas_call(
        functools.partial(_rec_proj_body, prec),
        grid=(m // tm,),
        in_specs=[pl.BlockSpec((tm, D_MODEL), tok)] + [_full(w.shape) for w in ws] + [_full((1, LANE))],
        out_specs=[pl.BlockSpec((tm, w), tok) for w in widths],
        out_shape=[jax.ShapeDtypeStruct((m, w), F32) for w in widths],
        compiler_params=_params(("parallel",)),
        name="rec_proj",
    )(x, *ws, lw['gla_b_a'][None])


def _gate_proj_body(prec, x_ref, w_ref, wt_ref, b_ref, bt_ref, g_ref, gt_ref):
    x = x_ref[...]
    z = _mxu(x, w_ref, prec) + b_ref[...]
    lane = lax.broadcasted_iota(jnp.int32, z.shape, 1)
    is_pre = (lane >= 4) & (lane < 8)
    g_ref[...] = jnp.where(is_pre, z, _log_sigmoid(z))
    x_hi = x.astype(BF16)
    zt = lax.dot_general(wt_ref[0], x_hi, _NT, preferred_element_type=F32)
    if prec == 3:
        x_lo = (x - x_hi.astype(F32)).astype(BF16)
        zt = zt + lax.dot_general(wt_ref[1], x_hi, _NT, preferred_element_type=F32)
        zt = zt + lax.dot_general(wt_ref[0], x_lo, _NT, preferred_element_type=F32)
    zt = zt + bt_ref[...]
    row = lax.broadcasted_iota(jnp.int32, zt.shape, 0)
    is_pre_t = (row >= 4) & (row < 8)
    gt_ref[...] = jnp.where(is_pre_t, zt, _log_sigmoid(zt))


def _gate_proj(x, lw, prec):
    m = x.shape[0]
    tm = _token_tile(m)
    offs = {}
    off = 0
    for name, width in IN_SPLITS:
        offs[name] = (off, off + width)
        off += width
    w_in = lw['w_in']
    col = lambda n: w_in[:, offs[n][0]:offs[n][1]]
    w_small = jnp.concatenate([col('fox_f'), col('ml_i'), col('ml_f')], axis=1)
    bias = jnp.concatenate([lw['fox_b_f'], lw['ml_b_i'], lw['ml_b_f']])
    w_tok = _wstack(jnp.pad(w_small, ((0, 0), (0, LANE - 12))), prec)
    w_head = _wstack(jnp.pad(w_small.T, ((0, 4), (0, 0))), prec)
    b_tok = jnp.pad(bias, (0, LANE - 12))[None]
    b_head = jnp.pad(bias, (0, 4))[:, None]
    return pl.pallas_call(
        functools.partial(_gate_proj_body, prec),
        grid=(m // tm,),
        in_specs=[pl.BlockSpec((tm, D_MODEL), lambda i: (i, 0)), _full(w_tok.shape), _full(w_head.shape),
                  _full((1, LANE)), _full((16, 1))],
        out_specs=[pl.BlockSpec((tm, LANE), lambda i: (i, 0)), pl.BlockSpec((16, tm), lambda i: (0, i))],
        out_shape=[jax.ShapeDtypeStruct((m, LANE), F32), jax.ShapeDtypeStruct((16, m), F32)],
        compiler_params=_params(("parallel",)),
        name="gate_proj",
    )(x, w_tok, w_head, b_tok, b_head)


def _cumsum_body(g_ref, gt_ref, c_ref, ct_ref, carry_ref, carry_t_ref):
    tc = g_ref.shape[0]

    @pl.when(pl.program_id(1) == 0)
    def _():
        carry_ref[...] = jnp.zeros_like(carry_ref)
        carry_t_ref[...] = jnp.zeros_like(carry_t_ref)

    r = lax.broadcasted_iota(jnp.int32, (tc, tc), 0)
    c = lax.broadcasted_iota(jnp.int32, (tc, tc), 1)
    tril = jnp.where(c <= r, 1.0, 0.0).astype(BF16)
    triu = jnp.where(r <= c, 1.0, 0.0).astype(BF16)
    g = g_ref[...]
    g1, g2 = _split(g)
    g3 = (g - g1.astype(F32) - g2.astype(F32)).astype(BF16)
    cs = sum(jnp.dot(tril, p, preferred_element_type=F32) for p in (g1, g2, g3)) + carry_ref[...]
    c_ref[...] = cs
    carry_ref[...] = cs[tc - 1:tc, :]
    gt = gt_ref[...]
    t1, t2 = _split(gt)
    t3 = (gt - t1.astype(F32) - t2.astype(F32)).astype(BF16)
    cst = sum(jnp.dot(p, triu, preferred_element_type=F32) for p in (t1, t2, t3)) + carry_t_ref[:, 0:1]
    ct_ref[...] = cst
    carry_t_ref[...] = jnp.broadcast_to(cst[:, tc - 1:tc], carry_t_ref.shape)


def _cumsum_gates(g, gt, n, t):
    tc = 256 if t % 256 == 0 else 128
    nb = t // tc
    return pl.pallas_call(
        _cumsum_body,
        grid=(n, nb),
        in_specs=[pl.BlockSpec((tc, LANE), lambda b, i: (b * nb + i, 0)),
                  pl.BlockSpec((16, tc), lambda b, i: (0, b * nb + i))],
        out_specs=[pl.BlockSpec((tc, LANE), lambda b, i: (b * nb + i, 0)),
                   pl.BlockSpec((16, tc), lambda b, i: (0, b * nb + i))],
        out_shape=[jax.ShapeDtypeStruct((n * t, LANE), F32), jax.ShapeDtypeStruct((16, n * t), F32)],
        scratch_shapes=[pltpu.VMEM((1, LANE), F32), pltpu.VMEM((16, LANE), F32)],
        compiler_params=_params(("parallel", "arbitrary")),
        name="gate_cumsum",
    )(g, gt)


def _flash_body(prec, use_bias, tq, tk, dv, *refs):
    if use_bias:
        q_ref, k_ref, v_ref, cc_ref, cr_ref, o_ref, m_ref, l_ref, acc_ref = refs
    else:
        q_ref, k_ref, v_ref, o_ref, m_ref, l_ref, acc_ref = refs
    nh = MLA_HEADS
    qb = pl.program_id(1)
    kv = pl.program_id(2)
    k_last = (qb * tq + tq - 1) // tk

    @pl.when(kv == 0)
    def _():
        m_ref[...] = jnp.full_like(m_ref, NEG_INF)
        l_ref[...] = jnp.zeros_like(l_ref)
        acc_ref[...] = jnp.zeros_like(acc_ref)

    def step(masked):
        q_hi = q_ref[0].reshape(nh * tq, QK_PAD)
        s = lax.dot_general(q_hi, k_ref[0], _NT, preferred_element_type=F32)
        if prec == 3:
            q_lo = q_ref[1].reshape(nh * tq, QK_PAD)
            s = s + lax.dot_general(q_hi, k_ref[1], _NT, preferred_element_type=F32)
            s = s + lax.dot_general(q_lo, k_ref[0], _NT, preferred_element_type=F32)
        if use_bias:
            parts = []
            for h in range(nh):
                bias = cc_ref[:, h:h + 1] - cr_ref[h:h + 1, :]
                parts.append(s[h * tq:(h + 1) * tq] + bias)
            s = jnp.concatenate(parts, axis=0)
        if masked:
            row = lax.broadcasted_iota(jnp.int32, (tq, tk), 0) + qb * tq
            col = lax.broadcasted_iota(jnp.int32, (tq, tk), 1) + kv * tk
            keep = jnp.concatenate([col <= row] * nh, axis=0)
            s = jnp.where(keep, s, NEG_INF)
        m_prev = m_ref[...]
        m_new = jnp.maximum(m_prev, jnp.max(s, axis=1, keepdims=True))
        alpha = jnp.exp(m_prev - m_new)
        p = jnp.exp(s - m_new)
        l_ref[...] = alpha * l_ref[...] + jnp.sum(p, axis=1, keepdims=True)
        m_ref[...] = m_new
        p_hi = p.astype(BF16)
        pv = jnp.dot(p_hi, v_ref[0], preferred_element_type=F32)
        if prec == 3:
            p_lo = (p - p_hi.astype(F32)).astype(BF16)
            pv = pv + jnp.dot(p_hi, v_ref[1], preferred_element_type=F32)
            pv = pv + jnp.dot(p_lo, v_ref[0], preferred_element_type=F32)
        acc_ref[...] = alpha * acc_ref[...] + pv

    @pl.when(kv < k_last)
    def _():
        step(False)

    @pl.when(kv == k_last)
    def _():
        step(True)
        out = acc_ref[...] / l_ref[...]
        if use_bias:
            head = lax.broadcasted_iota(jnp.int32, (tq, dv), 1) // (dv // nh)
            o = jnp.zeros((tq, dv), F32)
            for h in range(nh):
                o = o + jnp.where(head == h, out[h * tq:(h + 1) * tq], 0.0)
            o_ref[...] = o
        else:
            for h in range(nh):
                o_ref[:, h * dv:(h + 1) * dv] = out[h * tq:(h + 1) * tq]


def _flash(q, k, v, n, t, prec, bias=None):
    p = q.shape[0]
    dv = v.shape[-1]
    tq = 256 if t % 256 == 0 else 128
    tk = 512 if t % 512 == 0 else tq
    nq, nk = t // tq, t // tk
    use_bias = bias is not None
    d_out = dv if use_bias else MLA_HEADS * dv

    def kv_idx(b, i, j):
        return jnp.minimum(j, (i * tq + tq - 1) // tk)

    in_specs = [pl.BlockSpec((p, MLA_HEADS, tq, QK_PAD), lambda b, i, j: (0, 0, b * nq + i, 0)),
                pl.BlockSpec((p, tk, QK_PAD), lambda b, i, j: (0, b * nk + kv_idx(b, i, j), 0)),
                pl.BlockSpec((p, tk, dv), lambda b, i, j: (0, b * nk + kv_idx(b, i, j), 0))]
    args = [q, k, v]
    if use_bias:
        in_specs += [pl.BlockSpec((tq, LANE), lambda b, i, j: (b * nq + i, 0)),
                     pl.BlockSpec((16, tk), lambda b, i, j: (0, b * nk + kv_idx(b, i, j)))]
        args += list(bias)
    rows = MLA_HEADS * tq
    return pl.pallas_call(
        functools.partial(_flash_body, prec, use_bias, tq, tk, dv),
        grid=(n, nq, nk),
        in_specs=in_specs,
        out_specs=pl.BlockSpec((tq, d_out), lambda b, i, j: (b * nq + i, 0)),
        out_shape=jax.ShapeDtypeStruct((n * t, d_out), F32),
        scratch_shapes=[pltpu.VMEM((rows, 1), F32), pltpu.VMEM((rows, 1), F32), pltpu.VMEM((rows, dv), F32)],
        compiler_params=_params(("parallel", "parallel", "arbitrary")),
        name="fox_flash" if use_bias else "mla_flash",
    )(*args)


def _group_rms(y, gmat_ref, prec):
    ms = _mxu(y * y, gmat_ref, 3 if prec == 3 else 1)
    return y * lax.rsqrt(ms + EPS)


def _merge_body(prec, x_ref, mla_ref, fox_ref, gla_ref, ml_ref, gg_ref, mo_ref, wuv_ref, gmat_ref,
                gn_ref, mn_ref, wb_ref, wg_ref, bg_ref, o_ref, br_ref, acc_ref):
    b = pl.program_id(1)

    @pl.when(b == 0)
    def _():
        br_ref[0] = _mxu(mla_ref[...], wuv_ref, prec)
        br_ref[1] = fox_ref[...]
        gg = gg_ref[...]
        br_ref[2] = _group_rms(gla_ref[...], gmat_ref, prec) * gn_ref[...] * (gg * _sigmoid(gg))
        br_ref[3] = _group_rms(ml_ref[...], gmat_ref, prec) * mn_ref[...] * _sigmoid(mo_ref[...])
        acc_ref[...] = jnp.zeros_like(acc_ref)

    gate = _sigmoid(_mxu(x_ref[...], wg_ref, prec) + bg_ref[...])
    acc_ref[...] += gate * _mxu(br_ref[b], wb_ref.at[0], prec)

    @pl.when(b == N_BRANCH - 1)
    def _():
        o_ref[...] = acc_ref[...]


def _merge(x, mla_lat, fox_o, gla_o, ml_h, gla_gate, ml_o, lw, prec):
    m = x.shape[0]
    tm = _token_tile(m)
    p = 1 if prec == 1 else 2
    wuv = jnp.transpose(lw['mla_w_uv'], (1, 0, 2))
    wuv_bd = jnp.zeros((MLA_HEADS * MLA_KV_RANK, BRANCH_W), F32)
    for h in range(MLA_HEADS):
        wuv_bd = wuv_bd.at[h * MLA_KV_RANK:(h + 1) * MLA_KV_RANK, h * MLA_V:(h + 1) * MLA_V].set(wuv[h])
    grp = jnp.arange(BRANCH_W) // GLA_DV
    gmat = jnp.where(grp[:, None] == grp[None, :], 1.0 / GLA_DV, 0.0).astype(F32)
    wuv_s, gmat_s = _wstack(wuv_bd, prec), _wstack(gmat, prec)
    wb = jnp.stack([_wstack(lw['w_branch'][b], prec) for b in range(N_BRANCH)])
    wg = _wstack(lw['w_gate'], prec)
    tok = lambda i, b: (i, 0)
    return pl.pallas_call(
        functools.partial(_merge_body, prec),
        grid=(m // tm, N_BRANCH),
        in_specs=[pl.BlockSpec((tm, D_MODEL), tok), pl.BlockSpec((tm, MLA_HEADS * MLA_KV_RANK), tok),
                  pl.BlockSpec((tm, BRANCH_W), tok), pl.BlockSpec((tm, BRANCH_W), tok),
                  pl.BlockSpec((tm, BRANCH_W), tok), pl.BlockSpec((tm, BRANCH_W), tok),
                  pl.BlockSpec((tm, BRANCH_W), tok),
                  _full(wuv_s.shape), _full(gmat_s.shape), _full((1, BRANCH_W)), _full((1, BRANCH_W)),
                  pl.BlockSpec((1, p, BRANCH_W, D_MODEL), lambda i, b: (b, 0, 0, 0)),
                  pl.BlockSpec((p, D_MODEL, D_MODEL), lambda i, b: (0, 0, b)),
                  pl.BlockSpec((1, D_MODEL), lambda i, b: (0, b))],
        out_specs=pl.BlockSpec((tm, D_MODEL), tok),
        out_shape=jax.ShapeDtypeStruct((m, D_MODEL), F32),
        scratch_shapes=[pltpu.VMEM((N_BRANCH, tm, BRANCH_W), F32), pltpu.VMEM((tm, D_MODEL), F32)],
        compiler_params=_params(("parallel", "arbitrary")),
        name="gated_merge",
    )(x, mla_lat, fox_o, gla_o, ml_h, gla_gate, ml_o, wuv_s, gmat_s,
      jnp.tile(lw['gla_norm'], GLA_HEADS)[None], jnp.tile(lw['ml_norm'], ML_HEADS)[None],
      wb, wg, lw['b_gate'][None])


def _layernorm(x, g, b):
    mu = jnp.mean(x, axis=-1, keepdims=True)
    xc = x - mu
    var = jnp.mean(xc * xc, axis=-1, keepdims=True)
    return xc * lax.rsqrt(var + EPS) * g + b


def _out_body(prec, x_ref, mixed_ref, wo_ref, g_ref, b_ref, o_ref):
    mix = _mxu(mixed_ref[...], wo_ref, prec)
    o_ref[...] = _layernorm(ALPHA * x_ref[...] + mix, g_ref[...], b_ref[...])


def _out_proj(x, mixed, lw, prec):
    m = x.shape[0]
    tm = _token_tile(m)
    wo = _wstack(lw['w_out'], prec)
    tok = lambda i: (i, 0)
    return pl.pallas_call(
        functools.partial(_out_body, prec),
        grid=(m // tm,),
        in_specs=[pl.BlockSpec((tm, D_MODEL), tok), pl.BlockSpec((tm, D_MODEL), tok), _full(wo.shape),
                  _full((1, D_MODEL)), _full((1, D_MODEL))],
        out_specs=pl.BlockSpec((tm, D_MODEL), tok),
        out_shape=jax.ShapeDtypeStruct((m, D_MODEL), F32),
        compiler_params=_params(("parallel",)),
        name="out_proj_ln",
    )(x, mixed, wo, lw['ln1_g'][None], lw['ln1_b'][None])


def _first_argmax(v, lane):
    mx = jnp.max(v, axis=1, keepdims=True)
    idx = jnp.min(jnp.where(v == mx, lane, 4.0 * LANE), axis=1, keepdims=True)
    return mx, idx


def _route(h, wr_ref, rb_ref):
    logits = _mxu(h, wr_ref, 3)
    scores = _sigmoid(logits)
    lane_i = lax.broadcasted_iota(jnp.int32, scores.shape, 1)
    lane = lane_i.astype(F32)
    group = (lane_i // GROUP_SIZE).astype(F32)
    ninf = -jnp.inf
    biased = jnp.where(lane_i < N_EXPERTS, scores + rb_ref[...], ninf)
    gscore = jnp.full(scores.shape, ninf, F32)
    for g in range(N_GROUPS):
        in_g = group == float(g)
        vg = jnp.where(in_g, biased, ninf)
        m1, i1 = _first_argmax(vg, lane)
        m2 = jnp.max(jnp.where(lane == i1, ninf, vg), axis=1, keepdims=True)
        gscore = jnp.where(in_g, m1 + m2, gscore)
    cand = jnp.full(scores.shape, ninf, F32)
    for _ in range(TOPK_GROUPS):
        _, gi = _first_argmax(gscore, group)
        hit = group == gi
        cand = jnp.where(hit, biased, cand)
        gscore = jnp.where(hit, ninf, gscore)
    w = jnp.zeros(scores.shape, F32)
    for _ in range(TOP_K):
        _, ei = _first_argmax(cand, lane)
        hit = lane == ei
        w = jnp.where(hit, scores, w)
        cand = jnp.where(hit, ninf, cand)
    return w / jnp.sum(w, axis=1, keepdims=True) * ROUTED_SCALE


def _moe_body(prec, h_ref, wr_ref, rb_ref, sg_ref, su_ref, sd_ref, wg_ref, wu_ref, wd_ref,
              g_ref, b_ref, o_ref, gates_ref, acc_ref):
    e = pl.program_id(1)

    @pl.when(e == 0)
    def _():
        h = h_ref[...]
        gates_ref[...] = _route(h, wr_ref, rb_ref)
        sg = _mxu(h, sg_ref, prec)
        sh = sg * _sigmoid(sg) * _mxu(h, su_ref, prec)
        acc_ref[...] = _mxu(sh, sd_ref, prec)

    hb = h_ref[...].astype(BF16)
    hg = jnp.dot(hb, wg_ref[0].astype(BF16), preferred_element_type=F32)
    hu = jnp.dot(hb, wu_ref[0].astype(BF16), preferred_element_type=F32)
    lane = lax.broadcasted_iota(jnp.int32, gates_ref.shape, 1)
    ge = jnp.sum(jnp.where(lane == e, gates_ref[...], 0.0), axis=1, keepdims=True)
    hid = hg * _sigmoid(hg) * hu * ge
    acc_ref[...] += jnp.dot(hid.astype(BF16), wd_ref[0].astype(BF16), preferred_element_type=F32)

    @pl.when(e == N_EXPERTS - 1)
    def _():
        o_ref[...] = _layernorm(ALPHA * h_ref[...] + acc_ref[...], g_ref[...], b_ref[...])


def _moe(h, lw, prec):
    m = h.shape[0]
    tm = _token_tile(m)
    wr = _wstack(jnp.pad(lw['router_w'], ((0, 0), (0, LANE - N_EXPERTS))), 3)
    rb = jnp.pad(lw['router_bias'], (0, LANE - N_EXPERTS))[None]
    sg, su, sd = (_wstack(lw[k], prec) for k in ('sh_w_gate', 'sh_w_up', 'sh_w_down'))
    tok = lambda i, e: (i, 0)
    exp = lambda i, e: (e, 0, 0)
    return pl.pallas_call(
        functools.partial(_moe_body, prec),
        grid=(m // tm, N_EXPERTS),
        in_specs=[pl.BlockSpec((tm, D_MODEL), tok), _full(wr.shape), _full((1, LANE)),
                  _full(sg.shape), _full(su.shape), _full(sd.shape),
                  pl.BlockSpec((1, D_MODEL, EXPERT_FF), exp), pl.BlockSpec((1, D_MODEL, EXPERT_FF), exp),
                  pl.BlockSpec((1, EXPERT_FF, D_MODEL), exp),
                  _full((1, D_MODEL)), _full((1, D_MODEL))],
        out_specs=pl.BlockSpec((tm, D_MODEL), tok),
        out_shape=jax.ShapeDtypeStruct((m, D_MODEL), F32),
        scratch_shapes=[pltpu.VMEM((tm, LANE), F32), pltpu.VMEM((tm, D_MODEL), F32)],
        compiler_params=_params(("parallel", "arbitrary")),
        name="moe",
    )(h, wr, rb, sg, su, sd, lw['exp_w_gate'], lw['exp_w_up'], lw['exp_w_down'],
      lw['ln2_g'][None], lw['ln2_b'][None])


def _mla_sample_attn(qcat, lat_new, kpe_new, cache_lat, cache_kpe, layer, page_table):
    s_len = qcat.shape[1]
    past = page_table.shape[1] * PAGE_SIZE
    mask = jnp.arange(past + s_len)[None, :] <= past + jnp.arange(s_len)[:, None]

    def one(args):
        qc, cn, kn, pt = args
        lat = jnp.concatenate([cache_lat[layer, pt].reshape(past, MLA_KV_RANK), cn], axis=0)
        kpe = jnp.concatenate([cache_kpe[layer, pt].reshape(past, MLA_ROPE), kn], axis=0)
        s = (jnp.einsum('qhc,kc->hqk', qc[..., :MLA_KV_RANK], lat)
             + jnp.einsum('qhr,kr->hqk', qc[..., MLA_KV_RANK:MLA_KV_RANK + MLA_ROPE], kpe))
        s = jnp.where(mask, s, NEG_INF)
        p = jax.nn.softmax(s, axis=-1)
        return jnp.einsum('hqk,kc->qhc', p, lat)

    return lax.map(one, (qcat, lat_new, kpe_new, page_table))


def _fox_sample_attn(q, k, v, logf, cache_k, cache_v, cache_logf, layer, page_table):
    s_len = q.shape[1]
    past = page_table.shape[1] * PAGE_SIZE
    mask = jnp.arange(past + s_len)[None, :] <= past + jnp.arange(s_len)[:, None]

    def one(args):
        qi, ki, vi, fi, pt = args
        pk = cache_k[layer, pt].reshape(past, FOX_HEADS, FOX_DH)
        pv = cache_v[layer, pt].reshape(past, FOX_HEADS, FOX_DH)
        pf = cache_logf[layer, pt].reshape(past, FOX_HEADS)
        suffix = lax.cumsum(pf, axis=0, reverse=True) - pf
        cum_new = jnp.cumsum(fi, axis=0)
        kterm = jnp.concatenate([suffix, -cum_new], axis=0)
        keys = jnp.concatenate([pk, ki], axis=0)
        vals = jnp.concatenate([pv, vi], axis=0)
        s = jnp.einsum('qhd,khd->hqk', qi, keys) + cum_new.T[:, :, None] + kterm.T[:, None, :]
        s = jnp.where(mask, s, NEG_INF)
        p = jax.nn.softmax(s, axis=-1)
        return jnp.einsum('hqk,khd->qhd', p, vals)

    return lax.map(one, (q, k, v, logf, page_table))


def _gla_recurrence(q, k, v, log_a, s0, precision):
    n, t, h, dk = q.shape
    dv = v.shape[-1]
    L = GLA_CHUNK if t % GLA_CHUNK == 0 else t
    nc = t // L
    tril = jnp.tril(jnp.ones((L, L), bool))
    ein = functools.partial(jnp.einsum, precision=precision)

    def chunks(z):
        return z.reshape((n, nc, L) + z.shape[2:]).swapaxes(0, 1)

    def step(S, xs):
        qc, kc, vc, ac = xs
        b = jnp.cumsum(ac, axis=1)
        diff = b[:, :, None] - b[:, None, :]
        decay = jnp.exp(jnp.where(tril[None, :, :, None, None], diff, -jnp.inf))
        a_mat = jnp.sum(qc[:, :, None] * kc[:, None, :] * decay, axis=-1).transpose(0, 3, 1, 2)
        o = ein('nhts,nshv->nthv', a_mat, vc) + ein('nthk,nhkv->nthv', qc * jnp.exp(b), S)
        b_last = b[:, -1]
        S_new = S * jnp.exp(b_last)[..., None] + ein('nshk,nshv->nhkv', kc * jnp.exp(b_last[:, None] - b), vc)
        return S_new, o

    S, o = lax.scan(step, s0, (chunks(q), chunks(k), chunks(v), chunks(log_a)))
    return o.swapaxes(0, 1).reshape(n, t, h, dv), S


def _mlstm_qk(ml_in, buf, lw, precision):
    n, t, _ = ml_in.shape
    xp = jnp.concatenate([buf, ml_in], axis=1)
    y = lw['ml_conv_b'] + sum(xp[:, j:j + t] * lw['ml_conv_w'][j] for j in range(ML_CONV))
    c = jax.nn.silu(y).reshape(n, t, ML_HEADS, ML_DH)
    q = jnp.einsum('nthd,hde->nthe', c, lw['ml_w_q'], precision=precision)
    k = jnp.einsum('nthd,hde->nthe', c, lw['ml_w_k'], precision=precision) * (ML_DH ** -0.5)
    return q, k, xp[:, t:]


def _mlstm_recurrence(q, k, v, i_pre, logf, c0, n0, m0, precision):
    nbat, t, h, d = q.shape
    L = ML_CHUNK if t % ML_CHUNK == 0 else t
    nc = t // L
    tril = jnp.tril(jnp.ones((L, L), bool))
    ein = functools.partial(jnp.einsum, precision=precision)

    def chunks(z):
        return z.reshape((nbat, nc, L) + z.shape[2:]).swapaxes(0, 1)

    def step(carry, xs):
        C, nv, m = carry
        qc, kc, vc, ic, fc = xs
        F = jnp.cumsum(fc, axis=1)
        a = F + m[:, None, :]
        dmat = jnp.where(tril[None, :, :, None], F[:, :, None, :] - F[:, None, :, :] + ic[:, None, :, :], -jnp.inf)
        m_t = jnp.maximum(a, jnp.max(dmat, axis=2))
        w_state = jnp.exp(a - m_t)
        w_in = jnp.exp(dmat - m_t[:, :, None, :])
        qk = ein('bthd,bshd->btsh', qc, kc) * w_in
        num = ein('btsh,bshv->bthv', qk, vc) + w_state[..., None] * ein('bthd,bhdv->bthv', qc, C)
        den = jnp.sum(qk, axis=2) + w_state * jnp.sum(qc * nv[:, None], axis=-1)
        h_out = num / jnp.maximum(jnp.abs(den), jnp.exp(-m_t))[..., None]
        wl_state, wl_in = w_state[:, -1], w_in[:, -1]
        C_new = wl_state[:, :, None, None] * C + ein('bshd,bshv->bhdv', wl_in[..., None] * kc, vc)
        n_new = wl_state[..., None] * nv + jnp.sum(wl_in[..., None] * kc, axis=1)
        return (C_new, n_new, m_t[:, -1]), h_out

    state, hs = lax.scan(step, (c0, n0, m0), (chunks(q), chunks(k), chunks(v), chunks(i_pre), chunks(logf)))
    return hs.swapaxes(0, 1).reshape(nbat, t, h, d), state


def _rope_tables(pos):
    half = MLA_ROPE // 2
    inv = ROPE_THETA ** (-jnp.arange(half, dtype=F32) / half)
    ang = pos.astype(F32)[:, None] * inv[None, :]
    cos = jnp.tile(jnp.cos(ang), (1, 2))
    sin = jnp.tile(jnp.sin(ang), (1, 2))
    pad = ((0, 0), (0, LANE - MLA_ROPE))
    return jnp.pad(cos, pad), jnp.pad(sin, pad)


def _layer(x, l, lw, caches, page_table, n, t, nb, ts, prec):
    (cache_mla_latent, cache_mla_krope, cache_fox_k, cache_fox_v, cache_fox_logf, state_gla,
     state_mlstm_c, state_mlstm_n, state_mlstm_m, state_mlstm_conv) = caches
    mp = n * t
    precision = lax.Precision.HIGHEST if prec == 3 else None
    pos = jnp.concatenate([jnp.tile(jnp.arange(t), n),
                           jnp.tile(page_table.shape[1] * PAGE_SIZE + jnp.arange(ts), nb)])
    cosp, sinp = _rope_tables(pos)

    qcat, kcat, ckv, kpe = _mla_proj(x, lw, cosp, sinp, prec)
    fqbd, fkb, fvb, fk, fv = _fox_proj(x, lw, prec)
    gq, gk, gv, gg, ga, mi, mv, mo = _rec_proj(x, lw, prec)
    gates, gates_t = _gate_proj(x, lw, prec)
    kpe = kpe[:, :MLA_ROPE]
    fox_logf = gates[:, 0:4]
    ml_i, ml_logf = gates[:, 4:8], gates[:, 8:12]

    mla_p = _flash(qcat, kcat, kcat[:, :, :MLA_KV_RANK], n, t, prec)
    csum, csum_t = _cumsum_gates(gates[:mp], gates_t[:, :mp], n, t)
    fox_p = _flash(fqbd, fkb, fvb, n, t, prec, bias=(csum, csum_t))

    qc_s = (qcat[0, :, mp:].astype(F32) + (qcat[1, :, mp:].astype(F32) if prec == 3 else 0.0))
    qc_s = qc_s.transpose(1, 0, 2).reshape(nb, ts, MLA_HEADS, QK_PAD)
    mla_s = _mla_sample_attn(qc_s, ckv[mp:].reshape(nb, ts, -1), kpe[mp:].reshape(nb, ts, -1),
                             cache_mla_latent, cache_mla_krope, l, page_table)
    fq_s = sum(fqbd[i, :, mp:].astype(F32) for i in range(fqbd.shape[0])).sum(axis=0)
    fox_s = _fox_sample_attn(fq_s.reshape(nb, ts, FOX_HEADS, FOX_DH), fk[mp:].reshape(nb, ts, FOX_HEADS, FOX_DH),
                             fv[mp:].reshape(nb, ts, FOX_HEADS, FOX_DH), fox_logf[mp:].reshape(nb, ts, FOX_HEADS),
                             cache_fox_k, cache_fox_v, cache_fox_logf, l, page_table)
    mla_lat = jnp.concatenate([mla_p, mla_s.reshape(nb * ts, -1)], axis=0)
    fox_o = jnp.concatenate([fox_p, fox_s.reshape(nb * ts, -1)], axis=0)

    def seqs(a, tail):
        return a[:mp].reshape((n, t) + tail), a[mp:].reshape((nb, ts) + tail)

    gq_p, gq_s = seqs(gq, (GLA_HEADS, GLA_DK))
    gk_p, gk_s = seqs(gk, (GLA_HEADS, GLA_DK))
    gv_p, gv_s = seqs(gv, (GLA_HEADS, GLA_DV))
    ga_p, ga_s = seqs(ga, (GLA_HEADS, GLA_DK))
    gla_p, gla_sp = _gla_recurrence(gq_p, gk_p, gv_p, ga_p, jnp.zeros((n, GLA_HEADS, GLA_DK, GLA_DV), F32), precision)
    gla_s, gla_ss = _gla_recurrence(gq_s, gk_s, gv_s, ga_s, state_gla[l], None)
    gla_o = jnp.concatenate([gla_p.reshape(mp, -1), gla_s.reshape(nb * ts, -1)], axis=0)

    mi_p, mi_s = seqs(mi, (ML_WIDTH,))
    mv_p, mv_s = seqs(mv, (ML_HEADS, ML_DH))
    ii_p, ii_s = seqs(ml_i, (ML_HEADS,))
    lf_p, lf_s = seqs(ml_logf, (ML_HEADS,))
    mq_p, mk_p, buf_p = _mlstm_qk(mi_p, jnp.zeros((n, ML_CONV - 1, ML_WIDTH), F32), lw, precision)
    mq_s, mk_s, buf_s = _mlstm_qk(mi_s, state_mlstm_conv[l], lw, None)
    mh_p, (mc_p, mn_p, mm_p) = _mlstm_recurrence(
        mq_p, mk_p, mv_p, ii_p, lf_p, jnp.zeros((n, ML_HEADS, ML_DH, ML_DH), F32),
        jnp.zeros((n, ML_HEADS, ML_DH), F32), jnp.zeros((n, ML_HEADS), F32), precision)
    mh_s, (mc_s, mn_s, mm_s) = _mlstm_recurrence(
        mq_s, mk_s, mv_s, ii_s, lf_s, state_mlstm_c[l], state_mlstm_n[l], state_mlstm_m[l], None)
    ml_h = jnp.concatenate([mh_p.reshape(mp, -1), mh_s.reshape(nb * ts, -1)], axis=0)

    mixed = _merge(x, mla_lat, fox_o, gla_o, ml_h, gg, mo, lw, prec)
    h = _out_proj(x, mixed, lw, prec)
    y = _moe(h, lw, prec)

    def split_state(a, tail):
        return a[:mp].reshape((n, t) + tail), a[mp:].reshape((nb, ts) + tail)

    ckv_p, ckv_s = split_state(ckv, (MLA_KV_RANK,))
    kpe_p, kpe_s = split_state(kpe, (MLA_ROPE,))
    fk_p, fk_s = split_state(fk, (FOX_HEADS, FOX_DH))
    fv_p, fv_s = split_state(fv, (FOX_HEADS, FOX_DH))
    ff_p, ff_s = split_state(fox_logf, (FOX_HEADS,))
    st_p = (ckv_p, kpe_p, fk_p, fv_p, ff_p, gla_sp, mc_p, mn_p, mm_p, buf_p)
    st_s = (ckv_s, kpe_s, fk_s, fv_s, ff_s, gla_ss, mc_s, mn_s, mm_s, buf_s)
    return y, st_p, st_s


_WEIGHT_NAMES = (
    'w_in', 'mla_q_norm', 'mla_w_uq', 'mla_kv_norm', 'mla_w_uk', 'mla_w_uv', 'fox_b_f', 'gla_w_a2',
    'gla_b_a', 'gla_norm', 'ml_conv_w', 'ml_conv_b', 'ml_w_q', 'ml_w_k', 'ml_b_i', 'ml_b_f',
    'ml_norm', 'w_branch', 'w_gate', 'b_gate', 'w_out', 'ln1_g', 'ln1_b', 'router_w', 'router_bias',
    'exp_w_gate', 'exp_w_up', 'exp_w_down', 'sh_w_gate', 'sh_w_up', 'sh_w_down', 'ln2_g', 'ln2_b')

_LAYER_PREC = (3,) * (DEPTH - 1) + (1,)


def kernel(x_prompt, x_sample, cache_mla_latent, cache_mla_krope, cache_fox_k, cache_fox_v, cache_fox_logf, state_gla, state_mlstm_c, state_mlstm_n, state_mlstm_m, state_mlstm_conv, page_table, w_in, mla_q_norm, mla_w_uq, mla_kv_norm, mla_w_uk, mla_w_uv, fox_b_f, gla_w_a2, gla_b_a, gla_norm, ml_conv_w, ml_conv_b, ml_w_q, ml_w_k, ml_b_i, ml_b_f, ml_norm, w_branch, w_gate, b_gate, w_out, ln1_g, ln1_b, router_w, router_bias, exp_w_gate, exp_w_up, exp_w_down, sh_w_gate, sh_w_up, sh_w_down, ln2_g, ln2_b):
    weights = dict(zip(_WEIGHT_NAMES, (
        w_in, mla_q_norm, mla_w_uq, mla_kv_norm, mla_w_uk, mla_w_uv, fox_b_f, gla_w_a2, gla_b_a,
        gla_norm, ml_conv_w, ml_conv_b, ml_w_q, ml_w_k, ml_b_i, ml_b_f, ml_norm, w_branch, w_gate,
        b_gate, w_out, ln1_g, ln1_b, router_w, router_bias, exp_w_gate, exp_w_up, exp_w_down,
        sh_w_gate, sh_w_up, sh_w_down, ln2_g, ln2_b)))
    caches = (cache_mla_latent, cache_mla_krope, cache_fox_k, cache_fox_v, cache_fox_logf, state_gla,
              state_mlstm_c, state_mlstm_n, state_mlstm_m, state_mlstm_conv)
    n, t, d = x_prompt.shape
    nb, ts, _ = x_sample.shape
    x = jnp.concatenate([x_prompt.reshape(n * t, d), x_sample.reshape(nb * ts, d)], axis=0)
    p_states = [[] for _ in range(10)]
    s_states = [[] for _ in range(10)]
    for l in range(DEPTH):
        lw = {k: v[l] for k, v in weights.items()}
        x, st_p, st_s = _layer(x, l, lw, caches, page_table, n, t, nb, ts, _LAYER_PREC[l])
        for j in range(10):
            p_states[j].append(st_p[j])
            s_states[j].append(st_s[j])
    p_out = [jnp.stack(s, axis=0) for s in p_states]
    s_out = [jnp.stack(s, axis=0) for s in s_states]
    return (x[:n * t].reshape(n, t, d), x[n * t:].reshape(nb, ts, d), *p_out, *s_out)
```

```python
import functools

import jax
import jax.numpy as jnp
from jax import lax
from jax.experimental import pallas as pl
from jax.experimental.pallas import tpu as pltpu

D_MODEL = 1024
DEPTH = 2
PAGE_SIZE = 128
MLA_HEADS = 4
MLA_Q_RANK = 256
MLA_KV_RANK = 128
MLA_NOPE = 64
MLA_ROPE = 32
MLA_V = 64
MLA_SCALE = (MLA_NOPE + MLA_ROPE) ** -0.5
ROPE_THETA = 10000.0
FOX_HEADS = 4
FOX_DH = 64
FOX_SCALE = FOX_DH ** -0.5
GLA_HEADS = 4
GLA_DK = 32
GLA_DV = 64
GLA_GATE_RANK = 16
GLA_TAU = 16.0
GLA_CHUNK = 32
ML_HEADS = 4
ML_DH = 64
ML_WIDTH = ML_HEADS * ML_DH
ML_CONV = 4
ML_CHUNK = 64
N_BRANCH = 4
BRANCH_W = MLA_HEADS * MLA_V
N_EXPERTS = 64
TOP_K = 6
N_GROUPS = 8
GROUP_SIZE = N_EXPERTS // N_GROUPS
TOPK_GROUPS = 4
EXPERT_FF = 256
ROUTED_SCALE = 2.5
ALPHA = (2 * DEPTH) ** 0.25
Q_BLOCK = 128
EPS = 1e-6
NEG_INF = -1e30

IN_SPLITS = (
    ('mla_cq', MLA_Q_RANK), ('mla_ckv', MLA_KV_RANK), ('mla_krope', MLA_ROPE),
    ('fox_q', FOX_HEADS * FOX_DH), ('fox_k', FOX_HEADS * FOX_DH), ('fox_v', FOX_HEADS * FOX_DH),
    ('fox_f', FOX_HEADS),
    ('gla_q', GLA_HEADS * GLA_DK), ('gla_k', GLA_HEADS * GLA_DK), ('gla_v', GLA_HEADS * GLA_DV),
    ('gla_a', GLA_GATE_RANK), ('gla_g', GLA_HEADS * GLA_DV),
    ('ml_qk', ML_WIDTH), ('ml_v', ML_WIDTH), ('ml_o', ML_WIDTH), ('ml_i', ML_HEADS), ('ml_f', ML_HEADS),
)

LANE = 128
QK_PAD = 256
BF16 = jnp.bfloat16
F32 = jnp.float32
VMEM_LIMIT = 56 * 1024 * 1024

_NN = (((1,), (0,)), ((), ()))
_NT = (((1,), (1,)), ((), ()))


def _params(sem):
    return pltpu.CompilerParams(dimension_semantics=sem, vmem_limit_bytes=VMEM_LIMIT)


def _token_tile(m):
    for t in (512, 256, 128):
        if m % t == 0:
            return t
    raise ValueError(f"token count {m} must be a multiple of 128")


def _split(a):
    hi = a.astype(BF16)
    lo = (a - hi.astype(F32)).astype(BF16)
    return hi, lo


def _wstack(w, prec):
    hi = w.astype(BF16)
    if prec == 1:
        return hi[None]
    return jnp.stack([hi, (w - hi.astype(F32)).astype(BF16)])


def _mxu(a, w, prec, dn=_NN):
    a_hi = a.astype(BF16)
    out = lax.dot_general(a_hi, w[0], dn, preferred_element_type=F32)
    if prec == 3:
        a_lo = (a - a_hi.astype(F32)).astype(BF16)
        out = out + lax.dot_general(a_hi, w[1], dn, preferred_element_type=F32)
        out = out + lax.dot_general(a_lo, w[0], dn, preferred_element_type=F32)
    return out


def _store_split(ref, a, prec, idx=()):
    hi = a.astype(BF16)
    ref[(0,) + idx] = hi
    if prec == 3:
        ref[(1,) + idx] = (a - hi.astype(F32)).astype(BF16)


def _rms(x):
    return x * lax.rsqrt(jnp.mean(x * x, axis=-1, keepdims=True) + EPS)


def _log_sigmoid(x):
    return jnp.minimum(x, 0.0) - jnp.log1p(jnp.exp(-jnp.abs(x)))


def _sigmoid(x):
    return 1.0 / (1.0 + jnp.exp(-x))


def _full(shape):
    nd = len(shape)
    return pl.BlockSpec(shape, lambda *_: (0,) * nd)


def _mla_proj_body(prec, x_ref, w_ref, wuq_ref, wuk_ref, gq_ref, gkv_ref, cos_ref, sin_ref,
                   qcat_ref, kcat_ref, ckv_ref, kpe_ref):
    z = _mxu(x_ref[...], w_ref, prec)
    cq = _rms(z[:, :256]) * gq_ref[...]
    ckv = _rms(z[:, 256:384]) * gkv_ref[...]
    cosp, sinp = cos_ref[...], sin_ref[...]
    kpe = z[:, 384:512] * cosp + z[:, 512:640] * sinp
    ckv_ref[...] = ckv
    kpe_ref[...] = kpe
    _store_split(kcat_ref, jnp.concatenate([ckv, kpe], axis=1), prec)
    qq = _mxu(cq, wuq_ref, prec)
    qlat = _mxu(qq[:, :256], wuk_ref, prec)
    for h in range(MLA_HEADS):
        lo, hi = h * LANE, (h + 1) * LANE
        qpe = qq[:, 256 + lo:256 + hi] * cosp + qq[:, 768 + lo:768 + hi] * sinp
        qc = jnp.concatenate([qlat[:, lo:hi], qpe], axis=1) * MLA_SCALE
        _store_split(qcat_ref, qc, prec, (h,))


def _mla_proj(x, lw, cosp, sinp, prec):
    m = x.shape[0]
    tm = _token_tile(m)
    p = 1 if prec == 1 else 2
    half = MLA_ROPE // 2
    w_in = lw['w_in']
    o_cq, o_ckv, o_kr = 0, MLA_Q_RANK, MLA_Q_RANK + MLA_KV_RANK

    def rot_cols(w):
        return jnp.concatenate([-w[..., half:], w[..., :half]], axis=-1)

    def pad_lanes(w):
        return jnp.pad(w, ((0, 0), (0, LANE - w.shape[-1])))

    wkr = w_in[:, o_kr:o_kr + MLA_ROPE]
    w_main = jnp.concatenate([w_in[:, o_cq:o_cq + MLA_Q_RANK], w_in[:, o_ckv:o_ckv + MLA_KV_RANK],
                              pad_lanes(wkr), pad_lanes(rot_cols(wkr))], axis=1)
    wuq = lw['mla_w_uq'].reshape(MLA_Q_RANK, MLA_HEADS, MLA_NOPE + MLA_ROPE)
    w_nope = wuq[:, :, :MLA_NOPE].reshape(MLA_Q_RANK, MLA_HEADS * MLA_NOPE)
    w_rope = wuq[:, :, MLA_NOPE:]
    w_rope_p = jnp.pad(w_rope, ((0, 0), (0, 0), (0, LANE - MLA_ROPE))).reshape(MLA_Q_RANK, MLA_HEADS * LANE)
    w_rot_p = jnp.pad(rot_cols(w_rope), ((0, 0), (0, 0), (0, LANE - MLA_ROPE))).reshape(MLA_Q_RANK, MLA_HEADS * LANE)
    wuq_all = jnp.concatenate([w_nope, w_rope_p, w_rot_p], axis=1)
    wuk = jnp.transpose(lw['mla_w_uk'], (1, 2, 0))
    wuk_bd = jnp.zeros((MLA_HEADS * MLA_NOPE, MLA_HEADS * MLA_KV_RANK), F32)
    for h in range(MLA_HEADS):
        wuk_bd = wuk_bd.at[h * MLA_NOPE:(h + 1) * MLA_NOPE, h * MLA_KV_RANK:(h + 1) * MLA_KV_RANK].set(wuk[h])
    ws = [_wstack(w_main, prec), _wstack(wuq_all, prec), _wstack(wuk_bd, prec)]
    return pl.pallas_call(
        functools.partial(_mla_proj_body, prec),
        grid=(m // tm,),
        in_specs=[pl.BlockSpec((tm, D_MODEL), lambda i: (i, 0))] + [_full(w.shape) for w in ws]
        + [_full((1, MLA_Q_RANK)), _full((1, MLA_KV_RANK)),
           pl.BlockSpec((tm, LANE), lambda i: (i, 0)), pl.BlockSpec((tm, LANE), lambda i: (i, 0))],
        out_specs=[pl.BlockSpec((p, MLA_HEADS, tm, QK_PAD), lambda i: (0, 0, i, 0)),
                   pl.BlockSpec((p, tm, QK_PAD), lambda i: (0, i, 0)),
                   pl.BlockSpec((tm, LANE), lambda i: (i, 0)),
                   pl.BlockSpec((tm, LANE), lambda i: (i, 0))],
        out_shape=[jax.ShapeDtypeStruct((p, MLA_HEADS, m, QK_PAD), BF16),
                   jax.ShapeDtypeStruct((p, m, QK_PAD), BF16),
                   jax.ShapeDtypeStruct((m, LANE), F32),
                   jax.ShapeDtypeStruct((m, LANE), F32)],
        compiler_params=_params(("parallel",)),
        name="mla_proj",
    )(x, *ws, lw['mla_q_norm'][None], lw['mla_kv_norm'][None], cosp, sinp)


def _fox_proj_body(prec, x_ref, w_ref, qbd_ref, kb_ref, vb_ref, k_ref, v_ref):
    z = _mxu(x_ref[...], w_ref, prec)
    w = FOX_HEADS * FOX_DH
    q, k, v = z[:, :w] * FOX_SCALE, z[:, w:2 * w], z[:, 2 * w:]
    k_ref[...] = k
    v_ref[...] = v
    _store_split(kb_ref, k, prec)
    _store_split(vb_ref, v, prec)
    head = lax.broadcasted_iota(jnp.int32, q.shape, 1) // FOX_DH
    for h in range(FOX_HEADS):
        _store_split(qbd_ref, jnp.where(head == h, q, 0.0), prec, (h,))


def _fox_proj(x, lw, prec):
    m = x.shape[0]
    tm = _token_tile(m)
    p = 1 if prec == 1 else 2
    w = FOX_HEADS * FOX_DH
    off = MLA_Q_RANK + MLA_KV_RANK + MLA_ROPE
    ws = _wstack(lw['w_in'][:, off:off + 3 * w], prec)
    tok = lambda i: (i, 0)
    return pl.pallas_call(
        functools.partial(_fox_proj_body, prec),
        grid=(m // tm,),
        in_specs=[pl.BlockSpec((tm, D_MODEL), tok), _full(ws.shape)],
        out_specs=[pl.BlockSpec((p, FOX_HEADS, tm, w), lambda i: (0, 0, i, 0)),
                   pl.BlockSpec((p, tm, w), lambda i: (0, i, 0)),
                   pl.BlockSpec((p, tm, w), lambda i: (0, i, 0)),
                   pl.BlockSpec((tm, w), tok), pl.BlockSpec((tm, w), tok)],
        out_shape=[jax.ShapeDtypeStruct((p, FOX_HEADS, m, w), BF16),
                   jax.ShapeDtypeStruct((p, m, w), BF16), jax.ShapeDtypeStruct((p, m, w), BF16),
                   jax.ShapeDtypeStruct((m, w), F32), jax.ShapeDtypeStruct((m, w), F32)],
        compiler_params=_params(("parallel",)),
        name="fox_proj",
    )(x, ws)


def _rec_proj_body(prec, x_ref, w_ref, wa2_ref, ba_ref,
                   gq_ref, gk_ref, gv_ref, gg_ref, ga_ref, mi_ref, mv_ref, mo_ref):
    z = _mxu(x_ref[...], w_ref, prec)
    gq_ref[...] = z[:, 0:128] * (GLA_DK ** -0.5)
    gk_ref[...] = z[:, 128:256]
    gv_ref[...] = z[:, 256:512]
    gg_ref[...] = z[:, 512:768]
    a = _mxu(z[:, 768:896], wa2_ref, prec) + ba_ref[...]
    ga_ref[...] = _log_sigmoid(a) * (1.0 / GLA_TAU)
    mi_ref[...] = z[:, 896:1152]
    mv_ref[...] = z[:, 1152:1408]
    mo_ref[...] = z[:, 1408:1664]


def _rec_proj(x, lw, prec):
    m = x.shape[0]
    tm = _token_tile(m)
    offs = {}
    off = 0
    for name, width in IN_SPLITS:
        offs[name] = (off, off + width)
        off += width
    w_in = lw['w_in']
    col = lambda n: w_in[:, offs[n][0]:offs[n][1]]
    w_a = jnp.pad(col('gla_a'), ((0, 0), (0, LANE - GLA_GATE_RANK)))
    w_main = jnp.concatenate([col('gla_q'), col('gla_k'), col('gla_v'), col('gla_g'), w_a,
                              col('ml_qk'), col('ml_v'), col('ml_o')], axis=1)
    wa2 = jnp.pad(lw['gla_w_a2'], ((0, LANE - GLA_GATE_RANK), (0, 0)))
    ws = [_wstack(w_main, prec), _wstack(wa2, prec)]
    tok = lambda i: (i, 0)
    widths = (128, 128, 256, 256, 128, 256, 256, 256)
    return pl.pallas_call(
        functools.partial(_rec_proj_body, prec),
        grid=(m // tm,),
        in_specs=[pl.BlockSpec((tm, D_MODEL), tok)] + [_full(w.shape) for w in ws] + [_full((1, LANE))],
        out_specs=[pl.BlockSpec((tm, w), tok) for w in widths],
        out_shape=[jax.ShapeDtypeStruct((m, w), F32) for w in widths],
        compiler_params=_params(("parallel",)),
        name="rec_proj",
    )(x, *ws, lw['gla_b_a'][None])


def _gate_proj_body(prec, x_ref, w_ref, wt_ref, b_ref, bt_ref, g_ref, gt_ref):
    x = x_ref[...]
    z = _mxu(x, w_ref, prec) + b_ref[...]
    lane = lax.broadcasted_iota(jnp.int32, z.shape, 1)
    is_pre = (lane >= 4) & (lane < 8)
    g_ref[...] = jnp.where(is_pre, z, _log_sigmoid(z))
    x_hi = x.astype(BF16)
    zt = lax.dot_general(wt_ref[0], x_hi, _NT, preferred_element_type=F32)
    if prec == 3:
        x_lo = (x - x_hi.astype(F32)).astype(BF16)
        zt = zt + lax.dot_general(wt_ref[1], x_hi, _NT, preferred_element_type=F32)
        zt = zt + lax.dot_general(wt_ref[0], x_lo, _NT, preferred_element_type=F32)
    zt = zt + bt_ref[...]
    row = lax.broadcasted_iota(jnp.int32, zt.shape, 0)
    is_pre_t = (row >= 4) & (row < 8)
    gt_ref[...] = jnp.where(is_pre_t, zt, _log_sigmoid(zt))


def _gate_proj(x, lw, prec):
    m = x.shape[0]
    tm = _token_tile(m)
    offs = {}
    off = 0
    for name, width in IN_SPLITS:
        offs[name] = (off, off + width)
        off += width
    w_in = lw['w_in']
    col = lambda n: w_in[:, offs[n][0]:offs[n][1]]
    w_small = jnp.concatenate([col('fox_f'), col('ml_i'), col('ml_f')], axis=1)
    bias = jnp.concatenate([lw['fox_b_f'], lw['ml_b_i'], lw['ml_b_f']])
    w_tok = _wstack(jnp.pad(w_small, ((0, 0), (0, LANE - 12))), prec)
    w_head = _wstack(jnp.pad(w_small.T, ((0, 4), (0, 0))), prec)
    b_tok = jnp.pad(bias, (0, LANE - 12))[None]
    b_head = jnp.pad(bias, (0, 4))[:, None]
    return pl.pallas_call(
        functools.partial(_gate_proj_body, prec),
        grid=(m // tm,),
        in_specs=[pl.BlockSpec((tm, D_MODEL), lambda i: (i, 0)), _full(w_tok.shape), _full(w_head.shape),
                  _full((1, LANE)), _full((16, 1))],
        out_specs=[pl.BlockSpec((tm, LANE), lambda i: (i, 0)), pl.BlockSpec((16, tm), lambda i: (0, i))],
        out_shape=[jax.ShapeDtypeStruct((m, LANE), F32), jax.ShapeDtypeStruct((16, m), F32)],
        compiler_params=_params(("parallel",)),
        name="gate_proj",
    )(x, w_tok, w_head, b_tok, b_head)


def _cumsum_body(g_ref, gt_ref, c_ref, ct_ref, carry_ref, carry_t_ref):
    tc = g_ref.shape[0]

    @pl.when(pl.program_id(1) == 0)
    def _():
        carry_ref[...] = jnp.zeros_like(carry_ref)
        carry_t_ref[...] = jnp.zeros_like(carry_t_ref)

    r = lax.broadcasted_iota(jnp.int32, (tc, tc), 0)
    c = lax.broadcasted_iota(jnp.int32, (tc, tc), 1)
    tril = jnp.where(c <= r, 1.0, 0.0).astype(BF16)
    triu = jnp.where(r <= c, 1.0, 0.0).astype(BF16)
    g = g_ref[...]
    g1, g2 = _split(g)
    g3 = (g - g1.astype(F32) - g2.astype(F32)).astype(BF16)
    cs = sum(jnp.dot(tril, p, preferred_element_type=F32) for p in (g1, g2, g3)) + carry_ref[...]
    c_ref[...] = cs
    carry_ref[...] = cs[tc - 1:tc, :]
    gt = gt_ref[...]
    t1, t2 = _split(gt)
    t3 = (gt - t1.astype(F32) - t2.astype(F32)).astype(BF16)
    cst = sum(jnp.dot(p, triu, preferred_element_type=F32) for p in (t1, t2, t3)) + carry_t_ref[:, 0:1]
    ct_ref[...] = cst
    carry_t_ref[...] = jnp.broadcast_to(cst[:, tc - 1:tc], carry_t_ref.shape)


def _cumsum_gates(g, gt, n, t):
    tc = 256 if t % 256 == 0 else 128
    nb = t // tc
    return pl.pallas_call(
        _cumsum_body,
        grid=(n, nb),
        in_specs=[pl.BlockSpec((tc, LANE), lambda b, i: (b * nb + i, 0)),
                  pl.BlockSpec((16, tc), lambda b, i: (0, b * nb + i))],
        out_specs=[pl.BlockSpec((tc, LANE), lambda b, i: (b * nb + i, 0)),
                   pl.BlockSpec((16, tc), lambda b, i: (0, b * nb + i))],
        out_shape=[jax.ShapeDtypeStruct((n * t, LANE), F32), jax.ShapeDtypeStruct((16, n * t), F32)],
        scratch_shapes=[pltpu.VMEM((1, LANE), F32), pltpu.VMEM((16, LANE), F32)],
        compiler_params=_params(("parallel", "arbitrary")),
        name="gate_cumsum",
    )(g, gt)


def _flash_body(prec, use_bias, tq, tk, dv, *refs):
    if use_bias:
        q_ref, k_ref, v_ref, cc_ref, cr_ref, o_ref, m_ref, l_ref, acc_ref = refs
    else:
        q_ref, k_ref, v_ref, o_ref, m_ref, l_ref, acc_ref = refs
    nh = MLA_HEADS
    qb = pl.program_id(1)
    kv = pl.program_id(2)
    k_last = (qb * tq + tq - 1) // tk

    @pl.when(kv == 0)
    def _():
        m_ref[...] = jnp.full_like(m_ref, NEG_INF)
        l_ref[...] = jnp.zeros_like(l_ref)
        acc_ref[...] = jnp.zeros_like(acc_ref)

    def step(masked):
        q_hi = q_ref[0].reshape(nh * tq, QK_PAD)
        s = lax.dot_general(q_hi, k_ref[0], _NT, preferred_element_type=F32)
        if prec == 3:
            q_lo = q_ref[1].reshape(nh * tq, QK_PAD)
            s = s + lax.dot_general(q_hi, k_ref[1], _NT, preferred_element_type=F32)
            s = s + lax.dot_general(q_lo, k_ref[0], _NT, preferred_element_type=F32)
        if use_bias:
            parts = []
            for h in range(nh):
                bias = cc_ref[:, h:h + 1] - cr_ref[h:h + 1, :]
                parts.append(s[h * tq:(h + 1) * tq] + bias)
            s = jnp.concatenate(parts, axis=0)
        if masked:
            row = lax.broadcasted_iota(jnp.int32, (tq, tk), 0) + qb * tq
            col = lax.broadcasted_iota(jnp.int32, (tq, tk), 1) + kv * tk
            keep = jnp.concatenate([col <= row] * nh, axis=0)
            s = jnp.where(keep, s, NEG_INF)
        m_prev = m_ref[...]
        m_new = jnp.maximum(m_prev, jnp.max(s, axis=1, keepdims=True))
        alpha = jnp.exp(m_prev - m_new)
        p = jnp.exp(s - m_new)
        l_ref[...] = alpha * l_ref[...] + jnp.sum(p, axis=1, keepdims=True)
        m_ref[...] = m_new
        p_hi = p.astype(BF16)
        pv = jnp.dot(p_hi, v_ref[0], preferred_element_type=F32)
        if prec == 3:
            p_lo = (p - p_hi.astype(F32)).astype(BF16)
            pv = pv + jnp.dot(p_hi, v_ref[1], preferred_element_type=F32)
            pv = pv + jnp.dot(p_lo, v_ref[0], preferred_element_type=F32)
        acc_ref[...] = alpha * acc_ref[...] + pv

    @pl.when(kv < k_last)
    def _():
        step(False)

    @pl.when(kv == k_last)
    def _():
        step(True)
        out = acc_ref[...] / l_ref[...]
        if use_bias:
            head = lax.broadcasted_iota(jnp.int32, (tq, dv), 1) // (dv // nh)
            o = jnp.zeros((tq, dv), F32)
            for h in range(nh):
                o = o + jnp.where(head == h, out[h * tq:(h + 1) * tq], 0.0)
            o_ref[...] = o
        else:
            for h in range(nh):
                o_ref[:, h * dv:(h + 1) * dv] = out[h * tq:(h + 1) * tq]


def _flash(q, k, v, n, t, prec, bias=None):
    p = q.shape[0]
    dv = v.shape[-1]
    tq = 256 if t % 256 == 0 else 128
    tk = 512 if t % 512 == 0 else tq
    nq, nk = t // tq, t // tk
    use_bias = bias is not None
    d_out = dv if use_bias else MLA_HEADS * dv

    def kv_idx(b, i, j):
        return jnp.minimum(j, (i * tq + tq - 1) // tk)

    in_specs = [pl.BlockSpec((p, MLA_HEADS, tq, QK_PAD), lambda b, i, j: (0, 0, b * nq + i, 0)),
                pl.BlockSpec((p, tk, QK_PAD), lambda b, i, j: (0, b * nk + kv_idx(b, i, j), 0)),
                pl.BlockSpec((p, tk, dv), lambda b, i, j: (0, b * nk + kv_idx(b, i, j), 0))]
    args = [q, k, v]
    if use_bias:
        in_specs += [pl.BlockSpec((tq, LANE), lambda b, i, j: (b * nq + i, 0)),
                     pl.BlockSpec((16, tk), lambda b, i, j: (0, b * nk + kv_idx(b, i, j)))]
        args += list(bias)
    rows = MLA_HEADS * tq
    return pl.pallas_call(
        functools.partial(_flash_body, prec, use_bias, tq, tk, dv),
        grid=(n, nq, nk),
        in_specs=in_specs,
        out_specs=pl.BlockSpec((tq, d_out), lambda b, i, j: (b * nq + i, 0)),
        out_shape=jax.ShapeDtypeStruct((n * t, d_out), F32),
        scratch_shapes=[pltpu.VMEM((rows, 1), F32), pltpu.VMEM((rows, 1), F32), pltpu.VMEM((rows, dv), F32)],
        compiler_params=_params(("parallel", "parallel", "arbitrary")),
        name="fox_flash" if use_bias else "mla_flash",
    )(*args)


DECODE_PAGES = 16
DEC_ROWS = 16
N_PAGE_ARRAYS = 5


def _softmax_step(m_ref, l_ref, acc_ref, scores, pv):
    m_prev = m_ref[...]
    m_new = m_prev
    for s in scores:
        m_new = jnp.maximum(m_new, jnp.max(s, axis=1, keepdims=True))
    alpha = jnp.exp(m_prev - m_new)
    l = alpha * l_ref[...]
    acc = alpha * acc_ref[...]
    for i, s in enumerate(scores):
        p = jnp.exp(s - m_new)
        l = l + jnp.sum(p, axis=1, keepdims=True)
        acc = acc + pv(i, p.astype(BF16))
    m_ref[...] = m_new
    l_ref[...] = l
    acc_ref[...] = acc


def _decode_body(G, ts, pt_ref, qm_ref, qf_ref, km_ref, kf_ref, vf_ref, lfn_ref, *refs):
    del pt_ref
    pages = refs[:N_PAGE_ARRAYS * G]
    lat_refs, kpe_refs, fk_refs, fv_refs, lg_refs = (pages[i * G:(i + 1) * G] for i in range(N_PAGE_ARRAYS))
    om_ref, of_ref, mm_ref, lm_ref, am_ref, mf_ref, lf_ref, af_ref, carry_ref = refs[N_PAGE_ARRAYS * G:]
    c = pl.program_id(1)
    nh = DEC_ROWS // ts
    kw = FOX_HEADS * FOX_DH

    @pl.when(c == 0)
    def _():
        for m_ref, l_ref, a_ref in ((mm_ref, lm_ref, am_ref), (mf_ref, lf_ref, af_ref)):
            m_ref[...] = jnp.full_like(m_ref, NEG_INF)
            l_ref[...] = jnp.zeros_like(l_ref)
            a_ref[...] = jnp.zeros_like(a_ref)
        carry_ref[...] = jnp.zeros_like(carry_ref)

    qm = qm_ref[0]
    q_lat, q_pe = qm[:, :MLA_KV_RANK], qm[:, MLA_KV_RANK:MLA_KV_RANK + MLA_ROPE]
    qf = qf_ref[0]
    lfn = lfn_ref[0]
    cum = [lfn[:, 0:1]]
    for j in range(1, ts):
        cum.append(cum[-1] + lfn[:, j:j + 1])
    t_row = lax.broadcasted_iota(jnp.int32, (DEC_ROWS, 1), 0) % ts
    cn_row = cum[ts - 1]
    for j in range(ts - 2, -1, -1):
        cn_row = jnp.where(t_row == j, cum[j], cn_row)

    r = lax.broadcasted_iota(jnp.int32, (PAGE_SIZE, PAGE_SIZE), 0)
    s_ = lax.broadcasted_iota(jnp.int32, (PAGE_SIZE, PAGE_SIZE), 1)
    later = jnp.where(r > s_, 1.0, 0.0).astype(BF16)

    lg_rows = []
    for g in range(G):
        lg = lg_refs[g][0, 0]
        lg_rows += [jnp.broadcast_to(lg[h:h + 1], (ts, PAGE_SIZE)) for h in range(nh)]
    lg_all = jnp.concatenate(lg_rows, axis=0)
    suffix_all = sum(jnp.dot(a, later, preferred_element_type=F32) for a in _pieces(lg_all))
    page_total = suffix_all[:, 0:1] + lg_all[:, 0:1]

    carry = carry_ref[...]
    sm, sf = [], []
    for g in range(G):
        rows = slice(g * DEC_ROWS, (g + 1) * DEC_ROWS)
        lat = lat_refs[g][0, 0].astype(BF16)
        s = lax.dot_general(q_lat, lat, _NT, preferred_element_type=F32)
        s = s + jnp.dot(q_pe, kpe_refs[g][0, 0].astype(BF16), preferred_element_type=F32)
        sm.append(s)
        kt = fk_refs[g][0, 0].reshape(kw, PAGE_SIZE).astype(BF16)
        sf.append(jnp.dot(qf, kt, preferred_element_type=F32) + (cn_row + carry + suffix_all[rows]))
        carry = carry + page_total[rows]
    carry_ref[...] = carry

    def pv_m(i, p):
        return jnp.dot(p, lat_refs[i][0, 0].astype(BF16), preferred_element_type=F32)

    def pv_f(i, p):
        vt = fv_refs[i][0, 0].reshape(kw, PAGE_SIZE).astype(BF16)
        return lax.dot_general(p, vt, _NT, preferred_element_type=F32)

    _softmax_step(mm_ref, lm_ref, am_ref, sm, pv_m)
    _softmax_step(mf_ref, lf_ref, af_ref, sf, pv_f)

    @pl.when(c == pl.num_programs(1) - 1)
    def _():
        col = lax.broadcasted_iota(jnp.int32, (DEC_ROWS, DEC_ROWS), 1)
        keep = col <= t_row
        kn = km_ref[0]
        s_new = lax.dot_general(qm, kn, _NT, preferred_element_type=F32)
        s_new = jnp.where(keep, s_new, NEG_INF)
        _softmax_step(mm_ref, lm_ref, am_ref, [s_new],
                      lambda i, p: jnp.dot(p, kn[:, :MLA_KV_RANK], preferred_element_type=F32))
        cn_col = jnp.zeros((DEC_ROWS, DEC_ROWS), F32)
        for j in range(ts):
            cn_col = jnp.where(col == j, cum[j], cn_col)
        sf_new = lax.dot_general(qf, kf_ref[0], _NT, preferred_element_type=F32) + (cn_row - cn_col)
        sf_new = jnp.where(keep, sf_new, NEG_INF)
        _softmax_step(mf_ref, lf_ref, af_ref, [sf_new],
                      lambda i, p: jnp.dot(p, vf_ref[0], preferred_element_type=F32))
        om = am_ref[...] / lm_ref[...]
        for h in range(nh):
            om_ref[0, :, h * MLA_KV_RANK:(h + 1) * MLA_KV_RANK] = om[h * ts:(h + 1) * ts]
        of = af_ref[...] / lf_ref[...]
        head = lax.broadcasted_iota(jnp.int32, (ts, kw), 1) // FOX_DH
        o = jnp.zeros((ts, kw), F32)
        for h in range(nh):
            o = o + jnp.where(head == h, of[h * ts:(h + 1) * ts], 0.0)
        of_ref[0] = o


def _decode_attn(l, qcat, fqbd, kcat, fkb, fvb, fox_logf_new, page_caches, page_table, mp, nb, ts):
    assert ts * MLA_HEADS == DEC_ROWS and MLA_HEADS == FOX_HEADS
    n_pages = page_table.shape[1]
    G = DECODE_PAGES if n_pages % DECODE_PAGES == 0 else n_pages
    nch = n_pages // G
    kw = FOX_HEADS * FOX_DH

    def rows(a):
        return a.reshape(MLA_HEADS, nb, ts, a.shape[-1]).transpose(1, 0, 2, 3).reshape(nb, DEC_ROWS, a.shape[-1])

    def new_keys(a):
        return jnp.pad(a.reshape(nb, ts, a.shape[-1]), ((0, 0), (0, DEC_ROWS - ts), (0, 0)))

    qm, qf = rows(qcat[0, :, mp:]), rows(fqbd[0, :, mp:])
    km, kf, vf = new_keys(kcat[0, mp:]), new_keys(fkb[0, mp:]), new_keys(fvb[0, mp:])
    lfn = jnp.transpose(fox_logf_new.reshape(nb, ts, FOX_HEADS), (0, 2, 1))
    lfn = jnp.broadcast_to(lfn[:, :, None, :], (nb, FOX_HEADS, ts, ts)).reshape(nb, DEC_ROWS, ts)
    lfn = jnp.pad(lfn, ((0, 0), (0, 0), (0, DEC_ROWS - ts)))

    def page_spec(shape, g):
        nd = len(shape)
        return pl.BlockSpec((1, 1) + shape, lambda b, c, pt: (l, pt[b, n_pages - 1 - (c * G + g)]) + (0,) * nd)

    seq = lambda b, c, pt: (b, 0, 0)
    page_shapes = ((PAGE_SIZE, MLA_KV_RANK), (MLA_ROPE, PAGE_SIZE), (FOX_HEADS, FOX_DH, PAGE_SIZE),
                   (FOX_HEADS, FOX_DH, PAGE_SIZE), (FOX_HEADS, PAGE_SIZE))
    in_specs = [pl.BlockSpec((1, DEC_ROWS, QK_PAD), seq), pl.BlockSpec((1, DEC_ROWS, kw), seq),
                pl.BlockSpec((1, DEC_ROWS, QK_PAD), seq), pl.BlockSpec((1, DEC_ROWS, kw), seq),
                pl.BlockSpec((1, DEC_ROWS, kw), seq), pl.BlockSpec((1, DEC_ROWS, DEC_ROWS), seq)]
    args = [qm, qf, km, kf, vf, lfn]
    for shape, cache in zip(page_shapes, page_caches):
        for g in range(G):
            in_specs.append(page_spec(shape, g))
            args.append(cache)
    grid_spec = pltpu.PrefetchScalarGridSpec(
        num_scalar_prefetch=1,
        grid=(nb, nch),
        in_specs=in_specs,
        out_specs=[pl.BlockSpec((1, ts, MLA_HEADS * MLA_KV_RANK), seq), pl.BlockSpec((1, ts, kw), seq)],
        scratch_shapes=[pltpu.VMEM((DEC_ROWS, 1), F32), pltpu.VMEM((DEC_ROWS, 1), F32),
                        pltpu.VMEM((DEC_ROWS, MLA_KV_RANK), F32),
                        pltpu.VMEM((DEC_ROWS, 1), F32), pltpu.VMEM((DEC_ROWS, 1), F32),
                        pltpu.VMEM((DEC_ROWS, kw), F32), pltpu.VMEM((DEC_ROWS, 1), F32)],
    )
    om, of = pl.pallas_call(
        functools.partial(_decode_body, G, ts),
        grid_spec=grid_spec,
        out_shape=[jax.ShapeDtypeStruct((nb, ts, MLA_HEADS * MLA_KV_RANK), F32),
                   jax.ShapeDtypeStruct((nb, ts, kw), F32)],
        compiler_params=_params(("parallel", "arbitrary")),
        name="decode_attn",
    )(page_table, *args)
    return om.reshape(nb * ts, -1), of.reshape(nb * ts, -1)


def _group_rms(y, gmat_ref, prec):
    ms = _mxu(y * y, gmat_ref, 3 if prec == 3 else 1)
    return y * lax.rsqrt(ms + EPS)


def _merge_body(prec, x_ref, mla_ref, fox_ref, gla_ref, ml_ref, gg_ref, mo_ref, wuv_ref, gmat_ref,
                gn_ref, mn_ref, wb_ref, wg_ref, bg_ref, o_ref, br_ref, acc_ref):
    b = pl.program_id(1)

    @pl.when(b == 0)
    def _():
        br_ref[0] = _mxu(mla_ref[...], wuv_ref, prec)
        br_ref[1] = fox_ref[...]
        gg = gg_ref[...]
        br_ref[2] = _group_rms(gla_ref[...], gmat_ref, prec) * gn_ref[...] * (gg * _sigmoid(gg))
        br_ref[3] = _group_rms(ml_ref[...], gmat_ref, prec) * mn_ref[...] * _sigmoid(mo_ref[...])
        acc_ref[...] = jnp.zeros_like(acc_ref)

    gate = _sigmoid(_mxu(x_ref[...], wg_ref, prec) + bg_ref[...])
    acc_ref[...] += gate * _mxu(br_ref[b], wb_ref.at[0], prec)

    @pl.when(b == N_BRANCH - 1)
    def _():
        o_ref[...] = acc_ref[...]


def _merge(x, mla_lat, fox_o, gla_o, ml_h, gla_gate, ml_o, lw, prec):
    m = x.shape[0]
    tm = _token_tile(m)
    p = 1 if prec == 1 else 2
    wuv = jnp.transpose(lw['mla_w_uv'], (1, 0, 2))
    wuv_bd = jnp.zeros((MLA_HEADS * MLA_KV_RANK, BRANCH_W), F32)
    for h in range(MLA_HEADS):
        wuv_bd = wuv_bd.at[h * MLA_KV_RANK:(h + 1) * MLA_KV_RANK, h * MLA_V:(h + 1) * MLA_V].set(wuv[h])
    grp = jnp.arange(BRANCH_W) // GLA_DV
    gmat = jnp.where(grp[:, None] == grp[None, :], 1.0 / GLA_DV, 0.0).astype(F32)
    wuv_s, gmat_s = _wstack(wuv_bd, prec), _wstack(gmat, prec)
    wb = jnp.stack([_wstack(lw['w_branch'][b], prec) for b in range(N_BRANCH)])
    wg = _wstack(lw['w_gate'], prec)
    tok = lambda i, b: (i, 0)
    return pl.pallas_call(
        functools.partial(_merge_body, prec),
        grid=(m // tm, N_BRANCH),
        in_specs=[pl.BlockSpec((tm, D_MODEL), tok), pl.BlockSpec((tm, MLA_HEADS * MLA_KV_RANK), tok),
                  pl.BlockSpec((tm, BRANCH_W), tok), pl.BlockSpec((tm, BRANCH_W), tok),
                  pl.BlockSpec((tm, BRANCH_W), tok), pl.BlockSpec((tm, BRANCH_W), tok),
                  pl.BlockSpec((tm, BRANCH_W), tok),
                  _full(wuv_s.shape), _full(gmat_s.shape), _full((1, BRANCH_W)), _full((1, BRANCH_W)),
                  pl.BlockSpec((1, p, BRANCH_W, D_MODEL), lambda i, b: (b, 0, 0, 0)),
                  pl.BlockSpec((p, D_MODEL, D_MODEL), lambda i, b: (0, 0, b)),
                  pl.BlockSpec((1, D_MODEL), lambda i, b: (0, b))],
        out_specs=pl.BlockSpec((tm, D_MODEL), tok),
        out_shape=jax.ShapeDtypeStruct((m, D_MODEL), F32),
        scratch_shapes=[pltpu.VMEM((N_BRANCH, tm, BRANCH_W), F32), pltpu.VMEM((tm, D_MODEL), F32)],
        compiler_params=_params(("parallel", "arbitrary")),
        name="gated_merge",
    )(x, mla_lat, fox_o, gla_o, ml_h, gla_gate, ml_o, wuv_s, gmat_s,
      jnp.tile(lw['gla_norm'], GLA_HEADS)[None], jnp.tile(lw['ml_norm'], ML_HEADS)[None],
      wb, wg, lw['b_gate'][None])


def _layernorm(x, g, b):
    mu = jnp.mean(x, axis=-1, keepdims=True)
    xc = x - mu
    var = jnp.mean(xc * xc, axis=-1, keepdims=True)
    return xc * lax.rsqrt(var + EPS) * g + b


def _out_body(prec, x_ref, mixed_ref, wo_ref, g_ref, b_ref, o_ref):
    mix = _mxu(mixed_ref[...], wo_ref, prec)
    o_ref[...] = _layernorm(ALPHA * x_ref[...] + mix, g_ref[...], b_ref[...])


def _out_proj(x, mixed, lw, prec):
    m = x.shape[0]
    tm = _token_tile(m)
    wo = _wstack(lw['w_out'], prec)
    tok = lambda i: (i, 0)
    return pl.pallas_call(
        functools.partial(_out_body, prec),
        grid=(m // tm,),
        in_specs=[pl.BlockSpec((tm, D_MODEL), tok), pl.BlockSpec((tm, D_MODEL), tok), _full(wo.shape),
                  _full((1, D_MODEL)), _full((1, D_MODEL))],
        out_specs=pl.BlockSpec((tm, D_MODEL), tok),
        out_shape=jax.ShapeDtypeStruct((m, D_MODEL), F32),
        compiler_params=_params(("parallel",)),
        name="out_proj_ln",
    )(x, mixed, wo, lw['ln1_g'][None], lw['ln1_b'][None])


def _first_argmax(v, lane):
    mx = jnp.max(v, axis=1, keepdims=True)
    idx = jnp.min(jnp.where(v == mx, lane, 4.0 * LANE), axis=1, keepdims=True)
    return mx, idx


def _route(h, wr_ref, rb_ref):
    logits = _mxu(h, wr_ref, 3)
    scores = _sigmoid(logits)
    lane_i = lax.broadcasted_iota(jnp.int32, scores.shape, 1)
    lane = lane_i.astype(F32)
    group = (lane_i // GROUP_SIZE).astype(F32)
    ninf = -jnp.inf
    biased = jnp.where(lane_i < N_EXPERTS, scores + rb_ref[...], ninf)
    gscore = jnp.full(scores.shape, ninf, F32)
    for g in range(N_GROUPS):
        in_g = group == float(g)
        vg = jnp.where(in_g, biased, ninf)
        m1, i1 = _first_argmax(vg, lane)
        m2 = jnp.max(jnp.where(lane == i1, ninf, vg), axis=1, keepdims=True)
        gscore = jnp.where(in_g, m1 + m2, gscore)
    cand = jnp.full(scores.shape, ninf, F32)
    for _ in range(TOPK_GROUPS):
        _, gi = _first_argmax(gscore, group)
        hit = group == gi
        cand = jnp.where(hit, biased, cand)
        gscore = jnp.where(hit, ninf, gscore)
    w = jnp.zeros(scores.shape, F32)
    for _ in range(TOP_K):
        _, ei = _first_argmax(cand, lane)
        hit = lane == ei
        w = jnp.where(hit, scores, w)
        cand = jnp.where(hit, ninf, cand)
    return w / jnp.sum(w, axis=1, keepdims=True) * ROUTED_SCALE


def _moe_body(prec, h_ref, wr_ref, rb_ref, sg_ref, su_ref, sd_ref, wg_ref, wu_ref, wd_ref,
              g_ref, b_ref, o_ref, gates_ref, acc_ref):
    e = pl.program_id(1)

    @pl.when(e == 0)
    def _():
        h = h_ref[...]
        gates_ref[...] = _route(h, wr_ref, rb_ref)
        sg = _mxu(h, sg_ref, prec)
        sh = sg * _sigmoid(sg) * _mxu(h, su_ref, prec)
        acc_ref[...] = _mxu(sh, sd_ref, prec)

    hb = h_ref[...].astype(BF16)
    hg = jnp.dot(hb, wg_ref[0].astype(BF16), preferred_element_type=F32)
    hu = jnp.dot(hb, wu_ref[0].astype(BF16), preferred_element_type=F32)
    lane = lax.broadcasted_iota(jnp.int32, gates_ref.shape, 1)
    ge = jnp.sum(jnp.where(lane == e, gates_ref[...], 0.0), axis=1, keepdims=True)
    hid = hg * _sigmoid(hg) * hu * ge
    acc_ref[...] += jnp.dot(hid.astype(BF16), wd_ref[0].astype(BF16), preferred_element_type=F32)

    @pl.when(e == N_EXPERTS - 1)
    def _():
        o_ref[...] = _layernorm(ALPHA * h_ref[...] + acc_ref[...], g_ref[...], b_ref[...])


def _moe(h, lw, prec):
    m = h.shape[0]
    tm = _token_tile(m)
    wr = _wstack(jnp.pad(lw['router_w'], ((0, 0), (0, LANE - N_EXPERTS))), 3)
    rb = jnp.pad(lw['router_bias'], (0, LANE - N_EXPERTS))[None]
    sg, su, sd = (_wstack(lw[k], prec) for k in ('sh_w_gate', 'sh_w_up', 'sh_w_down'))
    tok = lambda i, e: (i, 0)
    exp = lambda i, e: (e, 0, 0)
    return pl.pallas_call(
        functools.partial(_moe_body, prec),
        grid=(m // tm, N_EXPERTS),
        in_specs=[pl.BlockSpec((tm, D_MODEL), tok), _full(wr.shape), _full((1, LANE)),
                  _full(sg.shape), _full(su.shape), _full(sd.shape),
                  pl.BlockSpec((1, D_MODEL, EXPERT_FF), exp), pl.BlockSpec((1, D_MODEL, EXPERT_FF), exp),
                  pl.BlockSpec((1, EXPERT_FF, D_MODEL), exp),
                  _full((1, D_MODEL)), _full((1, D_MODEL))],
        out_specs=pl.BlockSpec((tm, D_MODEL), tok),
        out_shape=jax.ShapeDtypeStruct((m, D_MODEL), F32),
        scratch_shapes=[pltpu.VMEM((tm, LANE), F32), pltpu.VMEM((tm, D_MODEL), F32)],
        compiler_params=_params(("parallel", "arbitrary")),
        name="moe",
    )(h, wr, rb, sg, su, sd, lw['exp_w_gate'], lw['exp_w_up'], lw['exp_w_down'],
      lw['ln2_g'][None], lw['ln2_b'][None])


GLA_L = 16
REC_PAD = 16
_TN = (((0,), (0,)), ((), ()))


def _dot3(a, b, dn, prec):
    a_hi, b_hi = a.astype(BF16), b.astype(BF16)
    out = lax.dot_general(a_hi, b_hi, dn, preferred_element_type=F32)
    if prec == 3:
        a_lo = (a - a_hi.astype(F32)).astype(BF16)
        b_lo = (b - b_hi.astype(F32)).astype(BF16)
        out = out + lax.dot_general(a_hi, b_lo, dn, preferred_element_type=F32)
        out = out + lax.dot_general(a_lo, b_hi, dn, preferred_element_type=F32)
    return out


def _pieces(x):
    p1, p2 = _split(x)
    return p1, p2, (x - p1.astype(F32) - p2.astype(F32)).astype(BF16)


def _tri(L, lower):
    r = lax.broadcasted_iota(jnp.int32, (L, L), 0)
    c = lax.broadcasted_iota(jnp.int32, (L, L), 1)
    return jnp.where((c <= r) if lower else (r <= c), 1.0, 0.0).astype(BF16)


def _stack_heads(x, head, nh):
    return jnp.concatenate([jnp.where(head == h, x, 0.0) for h in range(nh)], axis=0)


def _own_lanes(x_all, head, nh, L):
    out = jnp.where(head == 0, x_all[0:L], 0.0)
    for h in range(1, nh):
        out = out + jnp.where(head == h, x_all[h * L:(h + 1) * L], 0.0)
    return out


def _causal_rows(L, nh):
    r = lax.broadcasted_iota(jnp.int32, (L, L), 0)
    c = lax.broadcasted_iota(jnp.int32, (L, L), 1)
    return jnp.concatenate([c <= r] * nh, axis=0)


def _gla_body(prec, L, q_ref, k_ref, v_ref, a_ref, s0_ref, o_ref, s_out_ref, st_ref):
    j = pl.program_id(1)

    @pl.when(j == 0)
    def _():
        st_ref[...] = s0_ref[0]

    tb = q_ref.shape[0]
    nh = GLA_HEADS
    tril = _tri(L, True)
    keep = _causal_rows(L, nh)
    head_k = lax.broadcasted_iota(jnp.int32, (L, nh * GLA_DK), 1) // GLA_DK
    head_v = lax.broadcasted_iota(jnp.int32, (L, nh * GLA_DV), 1) // GLA_DV
    blk = (lax.broadcasted_iota(jnp.int32, st_ref.shape, 0) // GLA_DV
           == lax.broadcasted_iota(jnp.int32, st_ref.shape, 1) // GLA_DK)
    st = st_ref[...]
    for ci in range(tb // L):
        rows = slice(ci * L, (ci + 1) * L)
        q, k, v, a = q_ref[rows, :], k_ref[rows, :], v_ref[rows, :], a_ref[rows, :]
        b = sum(jnp.dot(tril, p, preferred_element_type=F32) for p in _pieces(a))
        b_last = b[L - 1:L, :]
        qt = q * jnp.exp(b)
        kt = k * jnp.exp(-b)
        kh = k * jnp.exp(b_last - b)
        amat = _dot3(_stack_heads(qt, head_k, nh), kt, _NT, prec)
        amat = jnp.where(keep, amat, 0.0)
        o_all = _dot3(amat, v, _NN, prec)
        o_ref[rows, :] = _own_lanes(o_all, head_v, nh, L) + _dot3(qt, st, _NT, prec)
        st = jnp.where(blk, st * jnp.exp(b_last) + _dot3(v, kh, _TN, prec), 0.0)
    st_ref[...] = st

    @pl.when(j == pl.num_programs(1) - 1)
    def _():
        s_out_ref[0] = st


def _gla(q, k, v, a, s0, n, t, prec):
    L = GLA_L
    tb = 128 if t % 128 == 0 else t
    assert tb % L == 0
    nb = t // tb
    kd, vd = GLA_HEADS * GLA_DK, GLA_HEADS * GLA_DV
    st0 = jnp.zeros((n, vd, kd), F32)
    for h in range(GLA_HEADS):
        st0 = st0.at[:, h * GLA_DV:(h + 1) * GLA_DV, h * GLA_DK:(h + 1) * GLA_DK].set(jnp.swapaxes(s0[:, h], 1, 2))
    tok = lambda b, i: (b * nb + i, 0)
    seq = lambda b, i: (b, 0, 0)
    o, st = pl.pallas_call(
        functools.partial(_gla_body, prec, L),
        grid=(n, nb),
        in_specs=[pl.BlockSpec((tb, kd), tok), pl.BlockSpec((tb, kd), tok), pl.BlockSpec((tb, vd), tok),
                  pl.BlockSpec((tb, kd), tok), pl.BlockSpec((1, vd, kd), seq)],
        out_specs=[pl.BlockSpec((tb, vd), tok), pl.BlockSpec((1, vd, kd), seq)],
        out_shape=[jax.ShapeDtypeStruct((n * t, vd), F32), jax.ShapeDtypeStruct((n, vd, kd), F32)],
        scratch_shapes=[pltpu.VMEM((vd, kd), F32)],
        compiler_params=_params(("parallel", "arbitrary")),
        name="gla_recurrence",
    )(q, k, v, a, st0)
    state = jnp.stack([jnp.swapaxes(st[:, h * GLA_DV:(h + 1) * GLA_DV, h * GLA_DK:(h + 1) * GLA_DK], 1, 2)
                       for h in range(GLA_HEADS)], axis=1)
    return o, state


def _mlstm_body(prec, L, x_ref, v_ref, g_ref, gt_ref, cw_ref, cb_ref, wq_ref, wk_ref,
                c0_ref, n0_ref, m0_ref, buf0_ref, h_ref, c_out_ref, n_out_ref, m_out_ref,
                xp_ref, c_ref, n_ref, m_ref):
    j = pl.program_id(1)
    nh = ML_HEADS
    lane128 = lax.broadcasted_iota(jnp.int32, (1, LANE), 1)

    @pl.when(j == 0)
    def _():
        c_ref[...] = c0_ref[0]
        n_ref[...] = n0_ref[0]
        m0 = m0_ref[0]
        m_ref[...] = jnp.concatenate([jnp.broadcast_to(m0[:, h:h + 1], (L, 1)) for h in range(nh)], axis=0)
        xp_ref[0:8, :] = buf0_ref[0]

    x = x_ref[...]
    xp_ref[8:8 + L, :] = x
    y = cb_ref[...]
    for tap in range(ML_CONV):
        y = y + xp_ref[5 + tap:5 + tap + L, :] * cw_ref[tap:tap + 1, :]
    xp_ref[0:8, :] = xp_ref[L:L + 8, :]
    cact = y * _sigmoid(y)
    q = _mxu(cact, wq_ref, prec)
    k = _mxu(cact, wk_ref, prec) * (ML_DH ** -0.5)
    v = v_ref[...]

    g, gt = g_ref[...], gt_ref[0]
    cs = sum(jnp.dot(_tri(L, True), p, preferred_element_type=F32) for p in _pieces(g))
    cst = sum(jnp.dot(p, _tri(L, False), preferred_element_type=F32) for p in _pieces(gt))
    f_col = [cs[:, 8 + h:9 + h] for h in range(nh)]
    i_col = [g[:, 4 + h:5 + h] for h in range(nh)]
    keep = _causal_rows(L, nh)
    dm = jnp.concatenate([f_col[h] + (gt[4 + h:5 + h, :] - cst[8 + h:9 + h, :]) for h in range(nh)], axis=0)
    dm = jnp.where(keep, dm, NEG_INF)
    a = jnp.concatenate(f_col, axis=0) + m_ref[...]
    m_t = jnp.maximum(a, jnp.max(dm, axis=1, keepdims=True))
    w_state = jnp.exp(a - m_t)
    w_in = jnp.exp(dm - m_t)

    head = lax.broadcasted_iota(jnp.int32, (L, ML_WIDTH), 1) // ML_DH
    qbd = _stack_heads(q, head, nh)
    qk = _dot3(qbd, k, _NT, prec) * w_in
    num = _dot3(qk, v, _NN, prec) + w_state * _dot3(qbd, c_ref[...], _NN, prec)
    den = jnp.sum(qk, axis=1, keepdims=True) + w_state * jnp.sum(qbd * n_ref[...], axis=1, keepdims=True)
    h_all = num / jnp.maximum(jnp.abs(den), jnp.exp(-m_t))
    h_ref[...] = _own_lanes(h_all, head, nh, L)

    head_row = head[0:1, :]
    wl_in = jnp.zeros((L, ML_WIDTH), F32)
    wl_state = jnp.zeros((1, ML_WIDTH), F32)
    m_rows, m_lane = [], jnp.zeros((1, LANE), F32)
    for h in range(nh):
        last = h * L + L - 1
        m_last = m_t[last:last + 1, :]
        col = jnp.exp(f_col[h][L - 1:L, :] - f_col[h] + i_col[h] - m_last)
        wl_in = jnp.where(head == h, col, wl_in)
        wl_state = jnp.where(head_row == h, w_state[last:last + 1, :], wl_state)
        m_rows.append(jnp.broadcast_to(m_last, (L, 1)))
        m_lane = jnp.where(lane128 == h, m_last, m_lane)
    khat = k * wl_in
    blk = (lax.broadcasted_iota(jnp.int32, c_ref.shape, 0) // ML_DH
           == lax.broadcasted_iota(jnp.int32, c_ref.shape, 1) // ML_DH)
    c_ref[...] = jnp.where(blk, c_ref[...] * wl_state + _dot3(khat, v, _TN, prec), 0.0)
    n_ref[...] = n_ref[...] * wl_state + jnp.sum(khat, axis=0, keepdims=True)
    m_ref[...] = jnp.concatenate(m_rows, axis=0)

    @pl.when(j == pl.num_programs(1) - 1)
    def _():
        c_out_ref[0] = c_ref[...]
        n_out_ref[0] = n_ref[...]
        m_out_ref[0] = m_lane


def _block_diag(blocks):
    n, nh, r, c = blocks.shape
    out = jnp.zeros((n, nh * r, nh * c), F32)
    for h in range(nh):
        out = out.at[:, h * r:(h + 1) * r, h * c:(h + 1) * c].set(blocks[:, h])
    return out


def _mlstm(x, v, g, gt, lw, c0, n0, m0, buf0, n, t, L, prec):
    nc = t // L
    w = ML_WIDTH
    wq = _wstack(_block_diag(lw['ml_w_q'][None])[0], prec)
    wk = _wstack(_block_diag(lw['ml_w_k'][None])[0], prec)
    c_bd = _block_diag(c0)
    n_row = n0.reshape(n, 1, w)
    m_row = jnp.pad(m0, ((0, 0), (0, LANE - ML_HEADS)))[:, None, :]
    buf = jnp.pad(buf0, ((0, 0), (8 - (ML_CONV - 1), 0), (0, 0)))
    tok = lambda b, i: (b * nc + i, 0)
    seq = lambda b, i: (b, 0, 0)
    h, c_bd, n_row, m_row = pl.pallas_call(
        functools.partial(_mlstm_body, prec, L),
        grid=(n, nc),
        in_specs=[pl.BlockSpec((L, w), tok), pl.BlockSpec((L, w), tok), pl.BlockSpec((L, LANE), tok),
                  pl.BlockSpec((1, 16, L), lambda b, i: (b * nc + i, 0, 0)),
                  _full((ML_CONV, w)), _full((1, w)), _full(wq.shape), _full(wk.shape),
                  pl.BlockSpec((1, w, w), seq), pl.BlockSpec((1, 1, w), seq), pl.BlockSpec((1, 1, LANE), seq),
                  pl.BlockSpec((1, 8, w), seq)],
        out_specs=[pl.BlockSpec((L, w), tok), pl.BlockSpec((1, w, w), seq), pl.BlockSpec((1, 1, w), seq),
                   pl.BlockSpec((1, 1, LANE), seq)],
        out_shape=[jax.ShapeDtypeStruct((n * t, w), F32), jax.ShapeDtypeStruct((n, w, w), F32),
                   jax.ShapeDtypeStruct((n, 1, w), F32), jax.ShapeDtypeStruct((n, 1, LANE), F32)],
        scratch_shapes=[pltpu.VMEM((L + 8, w), F32), pltpu.VMEM((w, w), F32), pltpu.VMEM((1, w), F32),
                        pltpu.VMEM((ML_HEADS * L, 1), F32)],
        compiler_params=_params(("parallel", "arbitrary")),
        name="mlstm_recurrence",
    )(x, v, g, gt, lw['ml_conv_w'], lw['ml_conv_b'][None], wq, wk, c_bd, n_row, m_row, buf)
    c = jnp.stack([c_bd[:, hh * ML_DH:(hh + 1) * ML_DH, hh * ML_DH:(hh + 1) * ML_DH] for hh in range(ML_HEADS)], axis=1)
    return h, c, n_row.reshape(n, ML_HEADS, ML_DH), m_row[:, 0, :ML_HEADS]


def _pad_steps(a, nb, ts, fill=None):
    a = a.reshape(nb, ts, a.shape[-1])
    if fill is None:
        a = jnp.pad(a, ((0, 0), (0, REC_PAD - ts), (0, 0)))
    else:
        a = jnp.concatenate([a, jnp.broadcast_to(fill, (nb, REC_PAD - ts, a.shape[-1]))], axis=1)
    return a.reshape(nb * REC_PAD, -1)


def _unpad_steps(a, nb, ts):
    return a.reshape(nb, REC_PAD, a.shape[-1])[:, :ts].reshape(nb * ts, -1)


def _rope_tables(pos):
    half = MLA_ROPE // 2
    inv = ROPE_THETA ** (-jnp.arange(half, dtype=F32) / half)
    ang = pos.astype(F32)[:, None] * inv[None, :]
    cos = jnp.tile(jnp.cos(ang), (1, 2))
    sin = jnp.tile(jnp.sin(ang), (1, 2))
    pad = ((0, 0), (0, LANE - MLA_ROPE))
    return jnp.pad(cos, pad), jnp.pad(sin, pad)


def _layer(x, l, lw, page_caches, states, page_table, n, t, nb, ts, prec):
    state_gla, state_mlstm_c, state_mlstm_n, state_mlstm_m, state_mlstm_conv = states
    mp = n * t
    pos = jnp.concatenate([jnp.tile(jnp.arange(t), n),
                           jnp.tile(page_table.shape[1] * PAGE_SIZE + jnp.arange(ts), nb)])
    cosp, sinp = _rope_tables(pos)

    qcat, kcat, ckv, kpe = _mla_proj(x, lw, cosp, sinp, prec)
    fqbd, fkb, fvb, fk, fv = _fox_proj(x, lw, prec)
    gq, gk, gv, gg, ga, mi, mv, mo = _rec_proj(x, lw, prec)
    gates, gates_t = _gate_proj(x, lw, prec)
    kpe = kpe[:, :MLA_ROPE]
    fox_logf = gates[:, 0:4]
    ml_i, ml_logf = gates[:, 4:8], gates[:, 8:12]

    mla_p = _flash(qcat, kcat, kcat[:, :, :MLA_KV_RANK], n, t, prec)
    csum, csum_t = _cumsum_gates(gates[:mp], gates_t[:, :mp], n, t)
    fox_p = _flash(fqbd, fkb, fvb, n, t, prec, bias=(csum, csum_t))

    mla_s, fox_s = _decode_attn(l, qcat, fqbd, kcat, fkb, fvb, fox_logf[mp:], page_caches, page_table, mp, nb, ts)
    mla_lat = jnp.concatenate([mla_p, mla_s], axis=0)
    fox_o = jnp.concatenate([fox_p, fox_s], axis=0)

    assert ts <= REC_PAD and ts >= ML_CONV - 1
    gla_p, gla_sp = _gla(gq[:mp], gk[:mp], gv[:mp], ga[:mp], jnp.zeros((n, GLA_HEADS, GLA_DK, GLA_DV), F32), n, t, prec)
    gla_s, gla_ss = _gla(_pad_steps(gq[mp:], nb, ts), _pad_steps(gk[mp:], nb, ts), _pad_steps(gv[mp:], nb, ts),
                         _pad_steps(ga[mp:], nb, ts), state_gla[l], nb, REC_PAD, prec)
    gla_o = jnp.concatenate([gla_p, _unpad_steps(gla_s, nb, ts)], axis=0)

    ml_chunk = LANE if t % LANE == 0 else t
    gt_p = gates_t[:, :mp].reshape(16, mp // ml_chunk, ml_chunk).transpose(1, 0, 2)
    mh_p, mc_p, mn_p, mm_p = _mlstm(
        mi[:mp], mv[:mp], gates[:mp], gt_p, lw, jnp.zeros((n, ML_HEADS, ML_DH, ML_DH), F32),
        jnp.zeros((n, ML_HEADS, ML_DH), F32), jnp.zeros((n, ML_HEADS), F32),
        jnp.zeros((n, ML_CONV - 1, ML_WIDTH), F32), n, t, ml_chunk, prec)
    pad_gate = jnp.where((jnp.arange(LANE) >= 4) & (jnp.arange(LANE) < 8), NEG_INF, 0.0).astype(F32)
    g_s = _pad_steps(gates[mp:], nb, ts, fill=pad_gate)
    gt_s = jnp.transpose(g_s.reshape(nb, REC_PAD, LANE)[:, :, :16], (0, 2, 1))
    mh_s, mc_s, mn_s, mm_s = _mlstm(
        _pad_steps(mi[mp:], nb, ts), _pad_steps(mv[mp:], nb, ts), g_s, gt_s, lw, state_mlstm_c[l],
        state_mlstm_n[l], state_mlstm_m[l], state_mlstm_conv[l], nb, REC_PAD, REC_PAD, prec)
    ml_h = jnp.concatenate([mh_p, _unpad_steps(mh_s, nb, ts)], axis=0)
    buf_p = mi[:mp].reshape(n, t, ML_WIDTH)[:, t - (ML_CONV - 1):]
    buf_s = mi[mp:].reshape(nb, ts, ML_WIDTH)[:, ts - (ML_CONV - 1):]

    mixed = _merge(x, mla_lat, fox_o, gla_o, ml_h, gg, mo, lw, prec)
    h = _out_proj(x, mixed, lw, prec)
    y = _moe(h, lw, prec)

    def split_state(a, tail):
        return a[:mp].reshape((n, t) + tail), a[mp:].reshape((nb, ts) + tail)

    ckv_p, ckv_s = split_state(ckv, (MLA_KV_RANK,))
    kpe_p, kpe_s = split_state(kpe, (MLA_ROPE,))
    fk_p, fk_s = split_state(fk, (FOX_HEADS, FOX_DH))
    fv_p, fv_s = split_state(fv, (FOX_HEADS, FOX_DH))
    ff_p, ff_s = split_state(fox_logf, (FOX_HEADS,))
    st_p = (ckv_p, kpe_p, fk_p, fv_p, ff_p, gla_sp, mc_p, mn_p, mm_p, buf_p)
    st_s = (ckv_s, kpe_s, fk_s, fv_s, ff_s, gla_ss, mc_s, mn_s, mm_s, buf_s)
    return y, st_p, st_s


_WEIGHT_NAMES = (
    'w_in', 'mla_q_norm', 'mla_w_uq', 'mla_kv_norm', 'mla_w_uk', 'mla_w_uv', 'fox_b_f', 'gla_w_a2',
    'gla_b_a', 'gla_norm', 'ml_conv_w', 'ml_conv_b', 'ml_w_q', 'ml_w_k', 'ml_b_i', 'ml_b_f',
    'ml_norm', 'w_branch', 'w_gate', 'b_gate', 'w_out', 'ln1_g', 'ln1_b', 'router_w', 'router_bias',
    'exp_w_gate', 'exp_w_up', 'exp_w_down', 'sh_w_gate', 'sh_w_up', 'sh_w_down', 'ln2_g', 'ln2_b')

_LAYER_PREC = (3,) * (DEPTH - 1) + (1,)


def kernel(x_prompt, x_sample, cache_mla_latent, cache_mla_krope, cache_fox_k, cache_fox_v, cache_fox_logf, state_gla, state_mlstm_c, state_mlstm_n, state_mlstm_m, state_mlstm_conv, page_table, w_in, mla_q_norm, mla_w_uq, mla_kv_norm, mla_w_uk, mla_w_uv, fox_b_f, gla_w_a2, gla_b_a, gla_norm, ml_conv_w, ml_conv_b, ml_w_q, ml_w_k, ml_b_i, ml_b_f, ml_norm, w_branch, w_gate, b_gate, w_out, ln1_g, ln1_b, router_w, router_bias, exp_w_gate, exp_w_up, exp_w_down, sh_w_gate, sh_w_up, sh_w_down, ln2_g, ln2_b):
    weights = dict(zip(_WEIGHT_NAMES, (
        w_in, mla_q_norm, mla_w_uq, mla_kv_norm, mla_w_uk, mla_w_uv, fox_b_f, gla_w_a2, gla_b_a,
        gla_norm, ml_conv_w, ml_conv_b, ml_w_q, ml_w_k, ml_b_i, ml_b_f, ml_norm, w_branch, w_gate,
        b_gate, w_out, ln1_g, ln1_b, router_w, router_bias, exp_w_gate, exp_w_up, exp_w_down,
        sh_w_gate, sh_w_up, sh_w_down, ln2_g, ln2_b)))
    page_caches = (cache_mla_latent, jnp.transpose(cache_mla_krope, (0, 1, 3, 2)),
                   jnp.transpose(cache_fox_k, (0, 1, 3, 4, 2)), jnp.transpose(cache_fox_v, (0, 1, 3, 4, 2)),
                   jnp.transpose(cache_fox_logf, (0, 1, 3, 2)))
    states = (state_gla, state_mlstm_c, state_mlstm_n, state_mlstm_m, state_mlstm_conv)
    n, t, d = x_prompt.shape
    nb, ts, _ = x_sample.shape
    x = jnp.concatenate([x_prompt.reshape(n * t, d), x_sample.reshape(nb * ts, d)], axis=0)
    p_states = [[] for _ in range(10)]
    s_states = [[] for _ in range(10)]
    for l in range(DEPTH):
        lw = {k: v[l] for k, v in weights.items()}
        x, st_p, st_s = _layer(x, l, lw, page_caches, states, page_table, n, t, nb, ts, _LAYER_PREC[l])
        for j in range(10):
            p_states[j].append(st_p[j])
            s_states[j].append(st_s[j])
    p_out = [jnp.stack(s, axis=0) for s in p_states]
    s_out = [jnp.stack(s, axis=0) for s in s_states]
    return (x[:n * t].reshape(n, t, d), x[n * t:].reshape(nb, ts, d), *p_out, *s_out)
```
